```python
import math
import jax
import jax.numpy as jnp
from jax import lax
import numpy as np

D_MODEL = 1024
BATCH = 2
SEQ = 8192
DEPTH = 2

GRID_W = 64
CTX_LEN = 256
EPS = 1e-6
MLP_HIDDEN = 4 * D_MODEL
N_MOD = 6
N_EVEN = (DEPTH + 1) // 2
N_ODD = DEPTH // 2

S5_WIDTH = D_MODEL // 2
S5_GROUP = 16
S5_GROUPS = S5_WIDTH // S5_GROUP
S5_STATE = 64
S5_LOG_DT_MIN = math.log(1e-3)
S5_LOG_DT_MAX = math.log(1e-1)
RET_WIDTH = D_MODEL // 2
RET_HEADS = 4
RET_HEAD_DIM = RET_WIDTH // RET_HEADS
RET_CHUNK = 128
ROPE_BASE = 10000.0
ROPE_PAIRS = RET_HEAD_DIM // 4
EVEN_IN = S5_WIDTH + 4 * RET_WIDTH
EVEN_MIX = S5_WIDTH + RET_WIDTH

ML_INNER = 2 * D_MODEL
ML_HEADS = 4
ML_HEAD_DIM = ML_INNER // ML_HEADS
ML_CHUNK = 128
ML_CONV_W = 5
ML_QKV_BLOCK = 4
ODD_IN = 2 * ML_INNER + 4 * ML_HEADS

kernel_name = 'hybrid_s5_retention_mlstm_prefix_dit'


def rms_norm(x, g):
    xf = x.astype(jnp.float32)
    y = xf * lax.rsqrt(jnp.mean(xf * xf, axis=-1, keepdims=True) + EPS)
    return (y * g.astype(jnp.float32)).astype(x.dtype)


def head_norm(x, g, n_heads):
    xh = x.reshape(x.shape[:-1] + (n_heads, x.shape[-1] // n_heads))
    xc = xh - jnp.mean(xh, axis=-1, keepdims=True)
    var = jnp.mean(xc * xc, axis=-1, keepdims=True)
    return (xc * lax.rsqrt(var + EPS)).reshape(x.shape) * g.astype(jnp.float32)


def modulate(h, shift, scale):
    return h * (1 + scale) + shift


def sq_relu_mlp(h, w1, w2):
    return jnp.square(jax.nn.relu(h @ w1)) @ w2


def split_heads(t, n_heads):
    b, l, w = t.shape
    return t.reshape(b, l, n_heads, w // n_heads).transpose(0, 2, 1, 3)


def merge_heads(t):
    b, h, l, dh = t.shape
    return t.transpose(0, 2, 1, 3).reshape(b, l, h * dh)


def flip_seq(t):
    return jnp.flip(t, 2)


def axial_rope(n_rows):
    row = jnp.repeat(jnp.arange(n_rows, dtype=jnp.float32), GRID_W)
    col = jnp.arange(n_rows * GRID_W) % GRID_W
    inv = ROPE_BASE ** (-jnp.arange(ROPE_PAIRS, dtype=jnp.float32) / ROPE_PAIRS)
    ang = jnp.concatenate([row[:, None] * inv, col.astype(jnp.float32)[:, None] * inv], axis=-1)
    return jnp.cos(ang), jnp.sin(ang)


def apply_rope(x, cos, sin):
    x1, x2 = jnp.split(x, 2, axis=-1)
    return jnp.concatenate([x1 * cos - x2 * sin, x1 * sin + x2 * cos], axis=-1)


def _linear_recurrence(e1, e2):
    a1, b1 = e1
    a2, b2 = e2
    return a1 * a2, a2 * b1 + b2


def s5_discretize(lam_re, lam_im, log_step, b_re, b_im):
    lam = lax.complex(lam_re.astype(jnp.float32), lam_im.astype(jnp.float32))
    step = jnp.exp(log_step.astype(jnp.float32))[:, None]
    lam_bar = jnp.exp(lam * step)
    b = lax.complex(b_re.astype(jnp.float32), b_im.astype(jnp.float32))
    b_bar = ((lam_bar - 1.0) / lam)[..., None] * b
    return lam_bar, b_bar


def s5_scan(u, lam_bar, b_bar, h0):
    bu = jnp.einsum('gpn,blgn->blgp', b_bar, u.astype(jnp.complex64))
    if h0 is not None:
        bu = bu.at[:, 0].add(lam_bar * h0)
    a = jnp.broadcast_to(lam_bar, (1, u.shape[1]) + lam_bar.shape)
    _, h = lax.associative_scan(_linear_recurrence, (a, bu), axis=1)
    return h


def s5_bidirectional(u_ctx, u_lat, lam_re, lam_im, log_step, b_re, b_im, c_re, c_im, d_skip):
    bsz = u_ctx.shape[0]
    ug_c = u_ctx.reshape(bsz, -1, S5_GROUPS, S5_GROUP)
    ug_l = u_lat.reshape(bsz, -1, S5_GROUPS, S5_GROUP)
    y_c = u_ctx * d_skip
    y_l = u_lat * d_skip
    for direction in range(2):
        lam_bar, b_bar = s5_discretize(lam_re[direction], lam_im[direction], log_step[direction],
                                       b_re[direction], b_im[direction])
        c_mat = lax.complex(c_re[direction].astype(jnp.float32), c_im[direction].astype(jnp.float32))
        seq_c = ug_c if direction == 0 else jnp.flip(ug_c, 1)
        seq_l = ug_l if direction == 0 else jnp.flip(ug_l, 1)
        h_c = s5_scan(seq_c, lam_bar, b_bar, None)
        h_l = s5_scan(seq_l, lam_bar, b_bar, h_c[:, -1])
        o_c = jnp.real(jnp.einsum('gnp,blgp->blgn', c_mat, h_c)).reshape(u_ctx.shape)
        o_l = jnp.real(jnp.einsum('gnp,blgp->blgn', c_mat, h_l)).reshape(u_lat.shape)
        if direction == 1:
            o_c = jnp.flip(o_c, 1)
            o_l = jnp.flip(o_l, 1)
        y_c = y_c + o_c
        y_l = y_l + o_l
    return y_c, y_l


def retention_scan(q, k, v, log_gamma, state0, strict):
    bsz, nh, seq, dk = q.shape
    dv = v.shape[-1]
    n_chunks = seq // RET_CHUNK

    def to_chunks(t):
        return jnp.moveaxis(t.reshape(bsz, nh, n_chunks, RET_CHUNK, t.shape[-1]), 2, 0)

    pos = jnp.arange(RET_CHUNK, dtype=jnp.float32)
    rel = pos[:, None] - pos[None, :]
    mask = rel > 0 if strict else rel >= 0
    lg = log_gamma[:, None, None]
    decay_intra = jnp.where(mask, jnp.exp(jnp.where(mask, rel, 0.0) * lg), 0.0)
    decay_query = jnp.exp((pos + 1.0) * lg[:, :, 0])[None, :, :, None]
    decay_key = jnp.exp((RET_CHUNK - 1.0 - pos) * lg[:, :, 0])[None, :, :, None]
    decay_chunk = jnp.exp(RET_CHUNK * log_gamma)[None, :, None, None]

    def step(state, inp):
        qj, kj, vj = inp
        scores = jnp.einsum('bhid,bhjd->bhij', qj, kj) * decay_intra
        out = (jnp.einsum('bhij,bhjv->bhiv', scores, vj)
               + jnp.einsum('bhid,bhdv->bhiv', qj, state) * decay_query)
        state = state * decay_chunk + jnp.einsum('bhjd,bhjv->bhdv', kj * decay_key, vj)
        return state, out

    state, out = lax.scan(step, state0, (to_chunks(q), to_chunks(k), to_chunks(v)))
    return jnp.moveaxis(out, 0, 2).reshape(bsz, nh, seq, dv), state


def retention_bidirectional(q_c, k_c, v_c, q_l, k_l, v_l, decay_logit, rope_cos, rope_sin):
    q_c, k_c, v_c, q_l, k_l, v_l = [split_heads(t, RET_HEADS) for t in (q_c, k_c, v_c, q_l, k_l, v_l)]
    q_l = apply_rope(q_l, rope_cos, rope_sin)
    k_l = apply_rope(k_l, rope_cos, rope_sin)
    k_c = k_c * RET_HEAD_DIM ** -0.5
    k_l = k_l * RET_HEAD_DIM ** -0.5
    log_gamma = jax.nn.log_sigmoid(decay_logit.astype(jnp.float32))
    zero = jnp.zeros((q_c.shape[0], RET_HEADS, RET_HEAD_DIM, RET_HEAD_DIM), jnp.float32)
    o_cf, s_cf = retention_scan(q_c, k_c, v_c, log_gamma[0], zero, False)
    o_lf, _ = retention_scan(q_l, k_l, v_l, log_gamma[0], s_cf, False)
    o_cb, s_cb = retention_scan(flip_seq(q_c), flip_seq(k_c), flip_seq(v_c), log_gamma[1], zero, True)
    o_lb, _ = retention_scan(flip_seq(q_l), flip_seq(k_l), flip_seq(v_l), log_gamma[1], s_cb, True)
    return merge_heads(o_cf + flip_seq(o_cb)), merge_heads(o_lf + flip_seq(o_lb))


def s5_retention_mixer(h_ctx, h_lat, w_in, w_out, lam_re, lam_im, log_step, b_re, b_im, c_re, c_im,
                       d_skip, w_glu, b_glu, decay_logit, gn_g, rope_cos, rope_sin, ctx_out):
    p_ctx = (h_ctx @ w_in).astype(jnp.float32)
    p_lat = (h_lat @ w_in).astype(jnp.float32)
    cuts = [S5_WIDTH + i * RET_WIDTH for i in range(4)]
    u_c, q_c, k_c, v_c, g_c = jnp.split(p_ctx, cuts, axis=-1)
    u_l, q_l, k_l, v_l, g_l = jnp.split(p_lat, cuts, axis=-1)
    s5_c, s5_l = s5_bidirectional(u_c, u_l, lam_re, lam_im, log_step, b_re, b_im, c_re, c_im, d_skip)
    ret_c, ret_l = retention_bidirectional(q_c, k_c, v_c, q_l, k_l, v_l, decay_logit, rope_cos, rope_sin)

    def combine(s5_y, ret_y, gate):
        a, b = jnp.split(jax.nn.gelu(s5_y) @ w_glu + b_glu, 2, axis=-1)
        s5_o = a * jax.nn.sigmoid(b)
        ret_o = head_norm(ret_y, gn_g, RET_HEADS) * jax.nn.silu(gate)
        return jnp.concatenate([s5_o, ret_o], axis=-1) @ w_out

    y_lat = combine(s5_l, ret_l, g_l).astype(h_lat.dtype)
    y_ctx = combine(s5_c, ret_c, g_c).astype(h_ctx.dtype) if ctx_out else None
    return y_ctx, y_lat


def centred_dwconv(x, w, b):
    width = w.shape[0]
    y = lax.conv_general_dilated(x, w.astype(x.dtype)[:, None, :], window_strides=(1,),
                                 padding=[(width // 2, width // 2)],
                                 dimension_numbers=('NWC', 'WIO', 'NWC'),
                                 feature_group_count=x.shape[-1])
    return y + b.astype(x.dtype)


def blockdiag_linear(x, w):
    bsz, seq, _ = x.shape
    xb = x.reshape(bsz, seq, w.shape[0], w.shape[1])
    return jnp.einsum('blni,nio->blno', xb, w.astype(x.dtype)).reshape(bsz, seq, -1)


def mlstm_scan(q, k, v, i_pre, log_f, state0):
    bsz, nh, seq, dk = q.shape
    dv = v.shape[-1]
    n_chunks = seq // ML_CHUNK

    def to_chunks(t):
        return jnp.moveaxis(t.reshape((bsz, nh, n_chunks, ML_CHUNK) + t.shape[3:]), 2, 0)

    tri = jnp.tril(jnp.ones((ML_CHUNK, ML_CHUNK), dtype=bool))

    def step(carry, inp):
        c_mat, n_vec, m_prev = carry
        qj, kj, vj, ij, fj = inp
        b = jnp.cumsum(fj, axis=-1)
        log_w = jnp.where(tri, b[..., :, None] - b[..., None, :] + ij[..., None, :], -jnp.inf)
        log_prev = b + m_prev[..., None]
        m_row = jnp.maximum(log_prev, jnp.max(log_w, axis=-1))
        w = jnp.exp(log_w - m_row[..., None])
        w_prev = jnp.exp(log_prev - m_row)
        s = jnp.einsum('bhid,bhjd->bhij', qj, kj) * w
        num = (jnp.einsum('bhij,bhjv->bhiv', s, vj)
               + w_prev[..., None] * jnp.einsum('bhid,bhdv->bhiv', qj, c_mat))
        den = jnp.sum(s, axis=-1) + w_prev * jnp.einsum('bhid,bhd->bhi', qj, n_vec)
        h = num / jnp.maximum(jnp.abs(den), jnp.exp(-m_row))[..., None]
        b_last = b[..., -1]
        log_k = b_last[..., None] - b + ij
        m_new = jnp.maximum(b_last + m_prev, jnp.max(log_k, axis=-1))
        w_k = jnp.exp(log_k - m_new[..., None])
        w_c = jnp.exp(b_last + m_prev - m_new)
        c_mat = w_c[..., None, None] * c_mat + jnp.einsum('bhjd,bhjv->bhdv', kj * w_k[..., None], vj)
        n_vec = w_c[..., None] * n_vec + jnp.einsum('bhj,bhjd->bhd', w_k, kj)
        return (c_mat, n_vec, m_new), h

    inputs = (to_chunks(q), to_chunks(k), to_chunks(v), to_chunks(i_pre), to_chunks(log_f))
    state, h = lax.scan(step, state0, inputs)
    return jnp.moveaxis(h, 0, 2).reshape(bsz, nh, seq, dv), state


def mlstm_mixer(h_ctx, h_lat, w_in, gate_b, conv_w, conv_b, wq, wk, wv, gn_g, skip, w_out, ctx_out):
    def prepare(h):
        p = (h @ w_in).astype(jnp.float32)
        xm, o_pre, gates = jnp.split(p, [ML_INNER, 2 * ML_INNER], axis=-1)
        xc = jax.nn.silu(centred_dwconv(xm, conv_w, conv_b))
        q = split_heads(blockdiag_linear(xc, wq), ML_HEADS)
        k = split_heads(blockdiag_linear(xc, wk), ML_HEADS) * ML_HEAD_DIM ** -0.5
        v = split_heads(blockdiag_linear(xm, wv), ML_HEADS)
        bsz, seq, _ = gates.shape
        gates = (gates + gate_b.astype(jnp.float32)).reshape(bsz, seq, 4, ML_HEADS).transpose(2, 0, 3, 1)
        return q, k, v, gates, xc, o_pre

    q_c, k_c, v_c, gt_c, xc_c, o_c = prepare(h_ctx)
    q_l, k_l, v_l, gt_l, xc_l, o_l = prepare(h_lat)
    bsz = q_c.shape[0]
    state0 = (jnp.zeros((bsz, ML_HEADS, ML_HEAD_DIM, ML_HEAD_DIM), jnp.float32),
              jnp.zeros((bsz, ML_HEADS, ML_HEAD_DIM), jnp.float32),
              jnp.zeros((bsz, ML_HEADS), jnp.float32))
    lsig = jax.nn.log_sigmoid
    hf_c, st_f = mlstm_scan(q_c, k_c, v_c, gt_c[0], lsig(gt_c[1]), state0)
    hf_l, _ = mlstm_scan(q_l, k_l, v_l, gt_l[0], lsig(gt_l[1]), st_f)
    hb_c, st_b = mlstm_scan(flip_seq(q_c), flip_seq(k_c), flip_seq(v_c),
                            flip_seq(gt_c[2]), flip_seq(lsig(gt_c[3])), state0)
    hb_l, _ = mlstm_scan(flip_seq(q_l), flip_seq(k_l), flip_seq(v_l),
                         flip_seq(gt_l[2]), flip_seq(lsig(gt_l[3])), st_b)

    def finish(h_f, h_b, xc, o_pre):
        h = merge_heads(h_f + flip_seq(h_b))
        h = head_norm(h, gn_g, ML_HEADS) + skip.astype(jnp.float32) * xc
        return (jax.nn.sigmoid(o_pre) * h) @ w_out

    y_lat = finish(hf_l, hb_l, xc_l, o_l).astype(h_lat.dtype)
    y_ctx = finish(hf_c, hb_c, xc_c, o_c).astype(h_ctx.dtype) if ctx_out else None
    return y_ctx, y_lat


def setup_inputs(seed: int = 0) -> dict:
    key = jax.random.key(seed)
    k = jax.random.split(key, 35)

    def nrm(i, shape, std):
        return std * jax.random.normal(k[i], shape, jnp.float32)

    d = D_MODEL
    g, p, n = S5_GROUPS, S5_STATE, S5_GROUP
    lam_im_base = jnp.pi * jnp.arange(p, dtype=jnp.float32)
    ret_logit_base = jnp.log(2.0 ** (5.0 + jnp.arange(RET_HEADS, dtype=jnp.float32)) - 1.0)
    f_bias = jnp.linspace(3.0, 6.0, ML_HEADS, dtype=jnp.float32)
    i_bias = jnp.zeros((ML_HEADS,), jnp.float32)
    gate_base = jnp.concatenate([i_bias, f_bias, i_bias, f_bias])
    return {
        'x': nrm(0, (BATCH, SEQ, d), 1.0),
        'c': nrm(1, (BATCH, d), 1.0),
        'ctx': nrm(2, (BATCH, CTX_LEN, d), 1.0),
        'c_ctx': nrm(3, (d,), 1.0),
        'mod_w': nrm(4, (DEPTH, d, N_MOD * d), 0.5 * d ** -0.5),
        'mod_b': nrm(5, (DEPTH, N_MOD * d), 0.02),
        'norm_mix_g': 1.0 + nrm(6, (DEPTH, d), 0.02),
        'norm_mlp_g': 1.0 + nrm(7, (DEPTH, d), 0.02),
        'mlp_w1': nrm(8, (DEPTH, d, MLP_HIDDEN), d ** -0.5),
        'mlp_w2': nrm(9, (DEPTH, MLP_HIDDEN, d), MLP_HIDDEN ** -0.5),
        'final_norm_g': 1.0 + nrm(10, (d,), 0.02),
        'ev_w_in': nrm(11, (N_EVEN, d, EVEN_IN), d ** -0.5),
        'ev_w_out': nrm(12, (N_EVEN, EVEN_MIX, d), EVEN_MIX ** -0.5),
        's5_lambda_re': -0.5 + nrm(13, (N_EVEN, 2, g, p), 0.01),
        's5_lambda_im': lam_im_base + nrm(14, (N_EVEN, 2, g, p), 0.01),
        's5_log_step': jax.random.uniform(k[15], (N_EVEN, 2, g), jnp.float32, S5_LOG_DT_MIN, S5_LOG_DT_MAX),
        's5_b_re': nrm(16, (N_EVEN, 2, g, p, n), (2 * n) ** -0.5),
        's5_b_im': nrm(17, (N_EVEN, 2, g, p, n), (2 * n) ** -0.5),
        's5_c_re': nrm(18, (N_EVEN, 2, g, n, p), (2 * p) ** -0.5),
        's5_c_im': nrm(19, (N_EVEN, 2, g, n, p), (2 * p) ** -0.5),
        's5_d': nrm(20, (N_EVEN, S5_WIDTH), 1.0),
        's5_w_glu': nrm(21, (N_EVEN, S5_WIDTH, 2 * S5_WIDTH), S5_WIDTH ** -0.5),
        's5_b_glu': nrm(22, (N_EVEN, 2 * S5_WIDTH), 0.02),
        'ret_decay_logit': ret_logit_base + nrm(23, (N_EVEN, 2, RET_HEADS), 0.01),
        'ret_gn_g': 1.0 + nrm(24, (N_EVEN, RET_WIDTH), 0.02),
        'ml_w_in': nrm(25, (N_ODD, d, ODD_IN), d ** -0.5),
        'ml_gate_b': gate_base + nrm(26, (N_ODD, 4 * ML_HEADS), 0.1),
        'ml_conv_w': nrm(27, (N_ODD, ML_CONV_W, ML_INNER), ML_CONV_W ** -0.5),
        'ml_conv_b': nrm(28, (N_ODD, ML_INNER), 0.02),
        'ml_wq': nrm(29, (N_ODD, ML_INNER // ML_QKV_BLOCK, ML_QKV_BLOCK, ML_QKV_BLOCK), ML_QKV_BLOCK ** -0.5),
        'ml_wk': nrm(30, (N_ODD, ML_INNER // ML_QKV_BLOCK, ML_QKV_BLOCK, ML_QKV_BLOCK), ML_QKV_BLOCK ** -0.5),
        'ml_wv': nrm(31, (N_ODD, ML_INNER // ML_QKV_BLOCK, ML_QKV_BLOCK, ML_QKV_BLOCK), ML_QKV_BLOCK ** -0.5),
        'ml_gn_g': 1.0 + nrm(32, (N_ODD, ML_INNER), 0.02),
        'ml_skip': 1.0 + nrm(33, (N_ODD, ML_INNER), 0.02),
        'ml_w_out': nrm(34, (N_ODD, ML_INNER, d), ML_INNER ** -0.5),
    }


def reference(x, c, ctx, c_ctx, mod_w, mod_b, norm_mix_g, norm_mlp_g, mlp_w1, mlp_w2, final_norm_g,
              ev_w_in, ev_w_out, s5_lambda_re, s5_lambda_im, s5_log_step, s5_b_re, s5_b_im, s5_c_re, s5_c_im,
              s5_d, s5_w_glu, s5_b_glu, ret_decay_logit, ret_gn_g,
              ml_w_in, ml_gate_b, ml_conv_w, ml_conv_b, ml_wq, ml_wk, ml_wv, ml_gn_g, ml_skip, ml_w_out):
    n_rows = x.shape[1] // GRID_W
    rope_cos, rope_sin = axial_rope(n_rows)
    silu_c = jax.nn.silu(c)
    silu_cc = jax.nn.silu(c_ctx)
    lat, cx = x, ctx
    for layer in range(DEPTH):
        ctx_out = layer < DEPTH - 1
        m_lat = [m[:, None, :] for m in jnp.split(silu_c @ mod_w[layer] + mod_b[layer], N_MOD, axis=-1)]
        m_ctx = jnp.split(silu_cc @ mod_w[layer] + mod_b[layer], N_MOD, axis=-1)
        h_lat = modulate(rms_norm(lat, norm_mix_g[layer]), m_lat[0], m_lat[1])
        h_ctx = modulate(rms_norm(cx, norm_mix_g[layer]), m_ctx[0], m_ctx[1])
        i = layer // 2
        if layer % 2 == 0:
            y_ctx, y_lat = s5_retention_mixer(
                h_ctx, h_lat, ev_w_in[i], ev_w_out[i], s5_lambda_re[i], s5_lambda_im[i], s5_log_step[i],
                s5_b_re[i], s5_b_im[i], s5_c_re[i], s5_c_im[i], s5_d[i], s5_w_glu[i], s5_b_glu[i],
                ret_decay_logit[i], ret_gn_g[i], rope_cos, rope_sin, ctx_out)
        else:
            y_ctx, y_lat = mlstm_mixer(
                h_ctx, h_lat, ml_w_in[i], ml_gate_b[i], ml_conv_w[i], ml_conv_b[i], ml_wq[i], ml_wk[i],
                ml_wv[i], ml_gn_g[i], ml_skip[i], ml_w_out[i], ctx_out)
        lat = lat + m_lat[2] * y_lat
        h2 = modulate(rms_norm(lat, norm_mlp_g[layer]), m_lat[3], m_lat[4])
        lat = lat + m_lat[5] * sq_relu_mlp(h2, mlp_w1[layer], mlp_w2[layer])
        if ctx_out:
            cx = cx + m_ctx[2] * y_ctx
            h2c = modulate(rms_norm(cx, norm_mlp_g[layer]), m_ctx[3], m_ctx[4])
            cx = cx + m_ctx[5] * sq_relu_mlp(h2c, mlp_w1[layer], mlp_w2[layer])
    return rms_norm(lat, final_norm_g)
```

```python
import functools
import math

import jax
import jax.numpy as jnp
from jax import lax
from jax.experimental import pallas as pl
from jax.experimental.pallas import tpu as pltpu

F32 = jnp.float32
BF16 = jnp.bfloat16

EPS = 1e-6
GRID_W = 64
ROPE_BASE = 10000.0
CHUNK = 128
ML_QKV_BLOCK = 4
V7X_SCOPED_VMEM_BYTES = 60000 * 1024
S5_SEGS = 4
S5_TB = 32
S5_PAIRS_PER_PASS = 4
ROW_TILE = 512
ML_ROW_TILE = 256
CONV_HALO = 8


def _cparams(sem, vmem_bytes=V7X_SCOPED_VMEM_BYTES):
    return pltpu.CompilerParams(dimension_semantics=sem, vmem_limit_bytes=vmem_bytes)


def _const_spec(shape):
    nd = len(shape)
    return pl.BlockSpec(shape, lambda *_: (0,) * nd, pipeline_mode=pl.Buffered(1))


def _dot(a, b):
    return jnp.dot(a, b, preferred_element_type=F32)


def _dot_nt(a, b):
    return lax.dot_general(a, b, (((1,), (1,)), ((), ())), preferred_element_type=F32)


def _dot_tn(a, b):
    return lax.dot_general(a, b, (((0,), (0,)), ((), ())), preferred_element_type=F32)


def _dot_f32(a, b):
    return jnp.dot(a, b, preferred_element_type=F32, precision=lax.Precision.HIGHEST)


def _sigmoid(x):
    return jax.nn.sigmoid(x)


def _silu(x):
    return x * jax.nn.sigmoid(x)


def _log_sigmoid(x):
    return jnp.minimum(x, 0.0) - jnp.log1p(jnp.exp(-jnp.abs(x)))


def _rms_mod(x, g, shift, scale):
    y = x * lax.rsqrt(jnp.mean(x * x, axis=-1, keepdims=True) + EPS) * g
    return y * (1.0 + scale) + shift


def _group_norm(x, width):
    outs = []
    for h in range(x.shape[-1] // width):
        xh = x[:, h * width:(h + 1) * width]
        xc = xh - jnp.mean(xh, axis=-1, keepdims=True)
        var = jnp.mean(xc * xc, axis=-1, keepdims=True)
        outs.append(xc * lax.rsqrt(var + EPS))
    return jnp.concatenate(outs, axis=-1)


class _Layout:
    def __init__(self, batch, ctx_len, seq):
        self.batch, self.ctx_len, self.seq = batch, ctx_len, seq
        self.ctx_rows = batch * ctx_len
        self.rows = batch * (ctx_len + seq)

    def mod_row(self, tile, tile_rows):
        n_ctx = self.ctx_rows // tile_rows
        return jnp.where(tile < n_ctx, self.batch, (tile - n_ctx) // (self.seq // tile_rows))

    def chunk_fwd(self, b, i):
        ncc, nlc = self.ctx_len // CHUNK, self.seq // CHUNK
        return jnp.where(i < ncc, b * ncc + i, self.batch * ncc + b * nlc + (i - ncc))

    def chunk_bwd(self, b, i):
        ncc, nlc = self.ctx_len // CHUNK, self.seq // CHUNK
        return jnp.where(i < ncc, b * ncc + (ncc - 1 - i),
                         self.batch * ncc + b * nlc + (nlc - 1 - (i - ncc)))


def _mod_kernel(c_ref, w_ref, b_ref, o_ref):
    s = _silu(c_ref[...]).astype(BF16)
    o_ref[0] = _dot(s, w_ref[0].astype(BF16)) + b_ref[0]


def _mod_table(c_all, mod_w, mod_b):
    depth, d, n = mod_w.shape
    tn = 1536
    out = pl.pallas_call(
        _mod_kernel,
        out_shape=jax.ShapeDtypeStruct((depth, c_all.shape[0], n), F32),
        grid=(depth, n // tn),
        in_specs=[pl.BlockSpec(c_all.shape, lambda l, j: (0, 0)),
                  pl.BlockSpec((1, d, tn), lambda l, j: (l, 0, j)),
                  pl.BlockSpec((1, 1, tn), lambda l, j: (l, 0, j))],
        out_specs=pl.BlockSpec((1, c_all.shape[0], tn), lambda l, j: (l, 0, j)),
        compiler_params=_cparams(("arbitrary", "arbitrary")),
        name="mod_table",
    )(c_all, mod_w, mod_b.reshape(depth, 1, n))
    return out.reshape(depth, c_all.shape[0], n // d, d)


def _proj_kernel(x_ref, m_ref, g_ref, w_ref, o_ref, *, tn):
    m = m_ref[0]
    h = _rms_mod(x_ref[...], g_ref[...], m[0:1], m[1:2]).astype(BF16)
    for c in range(o_ref.shape[1] // tn):
        o_ref[:, c * tn:(c + 1) * tn] = _dot(h, w_ref[:, c * tn:(c + 1) * tn])


def _proj(lay, stream, mods, g, w):
    rows, d = stream.shape
    n = w.shape[1]
    tm = ROW_TILE
    return pl.pallas_call(
        functools.partial(_proj_kernel, tn=512),
        out_shape=jax.ShapeDtypeStruct((rows, n), F32),
        grid=(rows // tm,),
        in_specs=[pl.BlockSpec((tm, d), lambda i: (i, 0)),
                  pl.BlockSpec((1,) + mods.shape[1:], lambda i: (lay.mod_row(i, tm), 0, 0)),
                  _const_spec((1, d)),
                  _const_spec(w.shape)],
        out_specs=pl.BlockSpec((tm, n), lambda i: (i, 0)),
        compiler_params=_cparams(("arbitrary",)),
        name="even_in_proj",
    )(stream, mods, g.reshape(1, d), w)


def _mlp_kernel(x_ref, m_ref, g_ref, w1_ref, w2_ref, fg_ref, o_ref, *, hc, final):
    x = x_ref[...]
    m = m_ref[0]
    h = _rms_mod(x, g_ref[...], m[3:4], m[4:5]).astype(BF16)
    acc = jnp.zeros(x.shape, F32)
    for c in range(w1_ref.shape[1] // hc):
        a = _dot(h, w1_ref[:, c * hc:(c + 1) * hc])
        a = jnp.square(jnp.maximum(a, 0.0)).astype(BF16)
        acc = acc + _dot(a, w2_ref[c * hc:(c + 1) * hc, :])
    y = x + m[5:6] * acc
    if final:
        y = y * lax.rsqrt(jnp.mean(y * y, axis=-1, keepdims=True) + EPS) * fg_ref[...]
    o_ref[...] = y


def _mlp(lay, stream, mods, g, w1, w2, final_g, *, latent_only, final):
    rows, d = stream.shape
    tm = ROW_TILE
    skip = lay.ctx_rows // tm if latent_only else 0
    n_tiles = rows // tm - skip
    return pl.pallas_call(
        functools.partial(_mlp_kernel, hc=512, final=final),
        out_shape=jax.ShapeDtypeStruct((n_tiles * tm, d), F32),
        grid=(n_tiles,),
        in_specs=[pl.BlockSpec((tm, d), lambda i: (i + skip, 0)),
                  pl.BlockSpec((1,) + mods.shape[1:], lambda i: (lay.mod_row(i + skip, tm), 0, 0)),
                  _const_spec((1, d)),
                  _const_spec(w1.shape),
                  _const_spec(w2.shape),
                  _const_spec((1, d))],
        out_specs=pl.BlockSpec((tm, d), lambda i: (i, 0)),
        compiler_params=_cparams(("arbitrary",)),
        name="mlp_final" if final else "mlp",
    )(stream, mods, g.reshape(1, d), w1, w2, final_g.reshape(1, d))


def _s5_disc_kernel(lre_ref, lim_ref, ls_ref, bre_ref, bim_ref,
                    lam_ref, bbar_ref, pow_ref, *, seg_ctx, seg_lat):
    lre, lim = lre_ref[...], lim_ref[...]
    step = jnp.exp(ls_ref[...])
    mag = jnp.exp(lre * step)
    ang = lim * step
    lbr, lbi = mag * jnp.cos(ang), mag * jnp.sin(ang)
    lam_ref[0], lam_ref[1] = lbr, lbi
    den = lre * lre + lim * lim
    nr, ni = lbr - 1.0, lbi
    cr = (nr * lre + ni * lim) / den
    ci = (ni * lre - nr * lim) / den
    bre, bim = bre_ref[...], bim_ref[...]
    bbar_ref[0] = cr[:, None] * bre - ci[:, None] * bim
    bbar_ref[1] = cr[:, None] * bim + ci[:, None] * bre
    for k, n in enumerate((seg_ctx, seg_lat)):
        mag_n = jnp.exp(lre * step * n)
        ang_n = lim * step * n
        pow_ref[2 * k] = mag_n * jnp.cos(ang_n)
        pow_ref[2 * k + 1] = mag_n * jnp.sin(ang_n)


def _s5_discretize(lam_re, lam_im, log_step, b_re, b_im, seg_ctx, seg_lat):
    _, g, p = lam_re.shape
    n = b_re.shape[-1]
    gp = g * p
    lt = gp // 128
    shp = (2, lt, 128)
    ls = jnp.broadcast_to(log_step[:, :, None], (2, g, p)).reshape(shp)
    bre = b_re.reshape(2, gp, n).transpose(0, 2, 1).reshape(2, n, lt, 128)
    bim = b_im.reshape(2, gp, n).transpose(0, 2, 1).reshape(2, n, lt, 128)
    lam, bbar, pw = pl.pallas_call(
        functools.partial(_s5_disc_kernel, seg_ctx=seg_ctx, seg_lat=seg_lat),
        out_shape=(jax.ShapeDtypeStruct((2,) + shp, F32),
                   jax.ShapeDtypeStruct((2, 2, n, lt, 128), F32),
                   jax.ShapeDtypeStruct((4,) + shp, F32)),
        name="s5_discretize",
    )(lam_re.reshape(shp), lam_im.reshape(shp), ls, bre, bim)
    return lam.reshape(2, 2, gp), bbar.reshape(2, 2, n, gp), pw.reshape(4, 2, gp)


def _s5_scan_kernel(*refs, tb, ncb, nblk, half, final_pass):
    if final_pass:
        (uf_ref, ub_ref, wb_ref, wc_ref, lam_ref, pow_ref, fin_ref,
         yf_ref, yb_ref, bu_f, bu_b, st_ref, init_ref) = refs
    else:
        uf_ref, ub_ref, wb_ref, lam_ref, fin_ref, bu_f, bu_b, st_ref = refs
    i = pl.program_id(0)
    part = jnp.where(i < ncb, 0, 1)
    first = jnp.logical_or(i == 0, i == ncb)
    last = jnp.logical_or(i == ncb - 1, i == nblk - 1)
    nseg = S5_SEGS
    nrow = st_ref.shape[1]

    if final_pass:
        @pl.when(i == 0)
        def _():
            for d in range(2):
                for b in range(nrow // nseg):
                    cr = jnp.zeros((1, half), F32)
                    ci = jnp.zeros((1, half), F32)
                    for prt in range(2):
                        pr = pow_ref[2 * prt, d:d + 1, :]
                        pi = pow_ref[2 * prt + 1, d:d + 1, :]
                        for s in range(nseg):
                            seg = s if d == 0 else nseg - 1 - s
                            r = b * nseg + seg
                            init_ref[d, prt, r:r + 1, 0:half] = cr
                            init_ref[d, prt, r:r + 1, half:2 * half] = ci
                            fr = fin_ref[d, prt, r:r + 1, 0:half]
                            fi = fin_ref[d, prt, r:r + 1, half:2 * half]
                            cr, ci = pr * cr - pi * ci + fr, pr * ci + pi * cr + fi

    @pl.when(first)
    def _():
        if final_pass:
            for d in range(2):
                st_ref[d] = jnp.where(part == 0, init_ref[d, 0], init_ref[d, 1])
        else:
            st_ref[...] = jnp.zeros(st_ref.shape, F32)

    kt_n = uf_ref.shape[1] // 128
    wcol = wb_ref.shape[3] // 2
    for d, (u_ref, bu_ref) in enumerate(((uf_ref, bu_f), (ub_ref, bu_b))):
        for kt in range(kt_n):
            r = _dot(u_ref[:, kt * 128:(kt + 1) * 128], wb_ref[d, kt])
            bu_ref[:, kt * wcol:(kt + 1) * wcol] = r[:, :wcol]
            bu_ref[:, half + kt * wcol:half + (kt + 1) * wcol] = r[:, wcol:]

    npair = half // 128
    ppp = S5_PAIRS_PER_PASS
    for d, bu_ref in enumerate((bu_f, bu_b)):
        for pb in range(npair // ppp):
            cols = [(pb * ppp + j) * 128 for j in range(ppp)]
            lr = [jnp.broadcast_to(lam_ref[0, d:d + 1, c:c + 128], (nrow, 128)) for c in cols]
            li = [jnp.broadcast_to(lam_ref[1, d:d + 1, c:c + 128], (nrow, 128)) for c in cols]
            hr0 = tuple(st_ref[d, :, c:c + 128] for c in cols)
            hi0 = tuple(st_ref[d, :, half + c:half + c + 128] for c in cols)

            def body(t, carry, d=d, bu_ref=bu_ref, cols=cols, lr=lr, li=li):
                hr, hi = carry
                tt = t if d == 0 else tb - 1 - t
                r0 = pl.multiple_of(tt * nrow, nrow)
                nhr, nhi = [], []
                for j, c in enumerate(cols):
                    br = bu_ref[pl.ds(r0, nrow), c:c + 128]
                    bi = bu_ref[pl.ds(r0, nrow), half + c:half + c + 128]
                    xr = lr[j] * hr[j] - li[j] * hi[j] + br
                    xi = lr[j] * hi[j] + li[j] * hr[j] + bi
                    if final_pass:
                        bu_ref[pl.ds(r0, nrow), c:c + 128] = xr
                        bu_ref[pl.ds(r0, nrow), half + c:half + c + 128] = xi
                    nhr.append(xr)
                    nhi.append(xi)
                return tuple(nhr), tuple(nhi)

            hr, hi = lax.fori_loop(0, tb, body, (hr0, hi0), unroll=4)
            for j, c in enumerate(cols):
                st_ref[d, :, c:c + 128] = hr[j]
                st_ref[d, :, half + c:half + c + 128] = hi[j]

    if final_pass:
        nt_n = wc_ref.shape[1]
        ncol = wc_ref.shape[3]
        krow = wc_ref.shape[2] // 2
        for d, (bu_ref, y_ref) in enumerate(((bu_f, yf_ref), (bu_b, yb_ref))):
            for nt in range(nt_n):
                h_re = bu_ref[:, nt * krow:(nt + 1) * krow].astype(BF16)
                h_im = bu_ref[:, half + nt * krow:half + (nt + 1) * krow].astype(BF16)
                y_ref[:, nt * ncol:(nt + 1) * ncol] = (
                    _dot(h_re, wc_ref[d, nt, 0:krow, :]) - _dot(h_im, wc_ref[d, nt, krow:2 * krow, :]))
    else:
        @pl.when(last)
        def _():
            for d in range(2):
                for prt in range(2):
                    @pl.when(part == prt)
                    def _(d=d, prt=prt):
                        fin_ref[d, prt] = st_ref[d]


def _s5_scan(u_tm, wb, wc, lam, pw, fin, *, ncb, final_pass):
    nrow = fin.shape[2]
    half = fin.shape[3] // 2
    tb = S5_TB
    blk = tb * nrow
    nblk = u_tm.shape[0] // blk
    uw = u_tm.shape[1]

    def f_idx(i):
        return (i, 0)

    def b_idx(i):
        return (jnp.where(i < ncb, ncb - 1 - i, nblk - 1 - (i - ncb)), 0)

    kern = functools.partial(_s5_scan_kernel, tb=tb, ncb=ncb, nblk=nblk, half=half,
                             final_pass=final_pass)
    scratch = [pltpu.VMEM((blk, 2 * half), F32), pltpu.VMEM((blk, 2 * half), F32),
               pltpu.VMEM((2, nrow, 2 * half), F32)]
    u_specs = [pl.BlockSpec((blk, uw), f_idx), pl.BlockSpec((blk, uw), b_idx)]
    if final_pass:
        return pl.pallas_call(
            kern,
            out_shape=(jax.ShapeDtypeStruct((u_tm.shape[0], wc.shape[1] * wc.shape[3]), F32),) * 2,
            grid=(nblk,),
            in_specs=u_specs + [_const_spec(wb.shape), _const_spec(wc.shape), _const_spec(lam.shape),
                                _const_spec(pw.shape), _const_spec(fin.shape)],
            out_specs=(pl.BlockSpec((blk, wc.shape[1] * wc.shape[3]), f_idx),
                       pl.BlockSpec((blk, wc.shape[1] * wc.shape[3]), b_idx)),
            scratch_shapes=scratch + [pltpu.VMEM(fin.shape, F32)],
            compiler_params=_cparams(("arbitrary",)),
            name="s5_scan_out",
        )(u_tm, u_tm, wb, wc, lam, pw, fin)
    return pl.pallas_call(
        kern,
        out_shape=jax.ShapeDtypeStruct(fin.shape, F32),
        grid=(nblk,),
        in_specs=u_specs + [_const_spec(wb.shape), _const_spec(lam.shape)],
        out_specs=pl.BlockSpec(fin.shape, lambda i: (0, 0, 0, 0)),
        scratch_shapes=scratch,
        compiler_params=_cparams(("arbitrary",)),
        name="s5_scan_state",
    )(u_tm, u_tm, wb, lam)


def _s5_mixer(lay, p0, lam_re, lam_im, log_step, b_re, b_im, c_re, c_im):
    bsz = lay.batch
    _, g, pst = lam_re.shape
    ngrp = b_re.shape[-1]
    uw = g * ngrp
    half = g * pst
    nseg = S5_SEGS
    seg_ctx, seg_lat = lay.ctx_len // nseg, lay.seq // nseg
    nrow = bsz * nseg

    lam, bbar, pw = _s5_discretize(lam_re, lam_im, log_step, b_re, b_im, seg_ctx, seg_lat)

    kt_g = 128 // ngrp
    kt_n = g // kt_g
    bb = bbar.reshape(2, 2, ngrp, kt_n, kt_g, pst)
    eye = jnp.eye(kt_g, dtype=bool)
    wb = jnp.where(eye[None, None, None, :, None, :, None],
                   bb.transpose(0, 1, 3, 2, 4, 5)[:, :, :, None, :, :, :], 0.0)
    wb = wb.reshape(2, 2, kt_n, kt_g * ngrp, kt_g * pst)
    wb = jnp.concatenate([wb[0], wb[1]], axis=-1).astype(BF16)

    nt_g = 256 // ngrp
    nt_n = g // nt_g
    eye_c = jnp.eye(nt_g, dtype=bool)

    def c_tiles(c):
        cc = c.reshape(2, nt_n, nt_g, ngrp, pst).transpose(0, 1, 2, 4, 3)
        w = jnp.where(eye_c[None, None, :, None, :, None], cc[:, :, :, :, None, :], 0.0)
        return w.reshape(2, nt_n, nt_g * pst, nt_g * ngrp)

    wc = jnp.concatenate([c_tiles(c_re), c_tiles(c_im)], axis=2).astype(BF16)

    u = p0[:, :uw].astype(BF16)
    u_c = u[:lay.ctx_rows].reshape(nrow, seg_ctx, uw).transpose(1, 0, 2)
    u_l = u[lay.ctx_rows:].reshape(nrow, seg_lat, uw).transpose(1, 0, 2)
    u_tm = jnp.concatenate([u_c, u_l], axis=0).reshape((seg_ctx + seg_lat) * nrow, uw)

    ncb = seg_ctx // S5_TB
    fin0 = jnp.zeros((2, 2, nrow, 2 * half), F32)
    fin = _s5_scan(u_tm, wb, None, lam, None, fin0, ncb=ncb, final_pass=False)
    yf, yb = _s5_scan(u_tm, wb, wc, lam, pw, fin, ncb=ncb, final_pass=True)

    def natural(y):
        y = y.reshape(seg_ctx + seg_lat, nrow, uw)
        y_c = y[:seg_ctx].transpose(1, 0, 2).reshape(lay.ctx_rows, uw)
        y_l = y[seg_ctx:].transpose(1, 0, 2).reshape(bsz * lay.seq, uw)
        return jnp.concatenate([y_c, y_l], axis=0)

    return natural(yf), natural(yb)


def _rope_tables(chunk_idx, col_cos, col_sin, is_lat):
    lane = lax.broadcasted_iota(jnp.int32, (8, 128), 1)
    sub = lax.broadcasted_iota(jnp.int32, (8, 128), 0)
    quarter = 32
    inv = jnp.exp((lane & (quarter - 1)).astype(F32) * (-math.log(ROPE_BASE) / quarter))
    rows_per_chunk = CHUNK // GRID_W
    ang = (chunk_idx * rows_per_chunk + sub).astype(F32) * inv
    rc, rs = jnp.cos(ang), jnp.sin(ang)
    tok = lax.broadcasted_iota(jnp.int32, (CHUNK, 128), 0)
    lane_t = lax.broadcasted_iota(jnp.int32, (CHUNK, 128), 1)
    row_cos = jnp.zeros((CHUNK, 128), F32)
    row_sin = jnp.zeros((CHUNK, 128), F32)
    for r in range(rows_per_chunk):
        sel = (tok // GRID_W) == r
        row_cos = jnp.where(sel, rc[r:r + 1, :], row_cos)
        row_sin = jnp.where(sel, rs[r:r + 1, :], row_sin)
    is_col = (lane_t & quarter) != 0
    cos_t = jnp.where(is_col, col_cos, row_cos)
    sin_t = jnp.where(is_col, col_sin, row_sin)
    sin_t = jnp.where(lane_t < 64, -sin_t, sin_t)
    cos_t = jnp.where(is_lat, cos_t, 1.0)
    sin_t = jnp.where(is_lat, sin_t, 0.0)
    return cos_t, sin_t


def _ret_kernel(dl_ref, qf_ref, kf_ref, vf_ref, qb_ref, kb_ref, vb_ref, of_ref, ob_ref,
                st_ref, tab_ref, col_ref, *, heads, ncc, nlc):
    i = pl.program_id(1)
    dh = qf_ref.shape[1] // heads
    ii = lax.broadcasted_iota(jnp.int32, (CHUNK, CHUNK), 0)
    jj = lax.broadcasted_iota(jnp.int32, (CHUNK, CHUNK), 1)

    @pl.when(i == 0)
    def _():
        st_ref[...] = jnp.zeros(st_ref.shape, F32)
        relf = (ii - jj).astype(F32)
        rowf = ii.astype(F32)
        for d in range(2):
            for h in range(heads):
                lg = _log_sigmoid(jnp.full((CHUNK, CHUNK), dl_ref[d, h], F32))
                if d == 0:
                    mask = ii >= jj
                    intra = jnp.where(mask, jnp.exp(jnp.where(mask, relf, 0.0) * lg), 0.0)
                    dq = jnp.exp((rowf + 1.0) * lg)
                    dk = jnp.exp((CHUNK - 1.0 - rowf) * lg)
                else:
                    mask = jj > ii
                    intra = jnp.where(mask, jnp.exp(jnp.where(mask, -relf, 0.0) * lg), 0.0)
                    dq = jnp.exp((CHUNK - rowf) * lg)
                    dk = jnp.exp(rowf * lg)
                tab_ref[d, h, 0] = intra
                tab_ref[d, h, 1] = dq
                tab_ref[d, h, 2] = dk
                tab_ref[d, h, 3] = jnp.exp(CHUNK * lg)
        lane = lax.broadcasted_iota(jnp.int32, (CHUNK, 128), 1)
        tok = lax.broadcasted_iota(jnp.int32, (CHUNK, 128), 0)
        inv = jnp.exp((lane & 31).astype(F32) * (-math.log(ROPE_BASE) / 32))
        ang = (tok % GRID_W).astype(F32) * inv
        col_ref[0] = jnp.cos(ang)
        col_ref[1] = jnp.sin(ang)

    is_lat = i >= ncc
    scale = dh ** -0.5
    for d, (q_ref, k_ref, v_ref, o_ref) in enumerate(((qf_ref, kf_ref, vf_ref, of_ref),
                                                      (qb_ref, kb_ref, vb_ref, ob_ref))):
        cidx = (i - ncc) if d == 0 else nlc - 1 - (i - ncc)
        cos_t, sin_t = _rope_tables(jnp.maximum(cidx, 0), col_ref[0], col_ref[1], is_lat)
        for h in range(heads):
            q = q_ref[:, h * dh:(h + 1) * dh]
            k = k_ref[:, h * dh:(h + 1) * dh]
            v = v_ref[:, h * dh:(h + 1) * dh].astype(BF16)
            q = q * cos_t + pltpu.roll(q, dh // 2, 1) * sin_t
            k = (k * cos_t + pltpu.roll(k, dh // 2, 1) * sin_t) * scale
            qb = q.astype(BF16)
            state = st_ref[d, h]
            s = _dot_nt(qb, k.astype(BF16)) * tab_ref[d, h, 0]
            o = _dot(s.astype(BF16), v) + _dot(qb, state.astype(BF16)) * tab_ref[d, h, 1]
            o_ref[:, h * dh:(h + 1) * dh] = o
            kd = (k * tab_ref[d, h, 2]).astype(BF16)
            st_ref[d, h] = state * tab_ref[d, h, 3] + _dot_tn(kd, v)


def _retention(lay, p0, decay_logit, width):
    heads = decay_logit.shape[1]
    dh = width // heads
    assert dh == 128 and CHUNK % GRID_W == 0
    ncc, nlc = lay.ctx_len // CHUNK, lay.seq // CHUNK
    ucols = (p0.shape[1] - 4 * width) // width

    def spec(col, idx):
        return pl.BlockSpec((CHUNK, width), lambda b, i: (idx(b, i), col))

    ins = [spec(ucols + c, f) for f in (lay.chunk_fwd, lay.chunk_bwd) for c in range(3)]
    outs = tuple(pl.BlockSpec((CHUNK, width), lambda b, i, f=f: (f(b, i), 0))
                 for f in (lay.chunk_fwd, lay.chunk_bwd))
    return pl.pallas_call(
        functools.partial(_ret_kernel, heads=heads, ncc=ncc, nlc=nlc),
        out_shape=(jax.ShapeDtypeStruct((lay.rows, width), F32),) * 2,
        grid=(lay.batch, ncc + nlc),
        in_specs=[pl.BlockSpec(memory_space=pltpu.SMEM)] + ins,
        out_specs=outs,
        scratch_shapes=[pltpu.VMEM((2, heads, dh, dh), F32),
                        pltpu.VMEM((2, heads, 4, CHUNK, CHUNK), F32),
                        pltpu.VMEM((2, CHUNK, 128), F32)],
        compiler_params=_cparams(("arbitrary", "arbitrary")),
        name="retention",
    )(decay_logit, p0, p0, p0, p0, p0, p0)


def _even_out_kernel(x_ref, m_ref, u_ref, yf_ref, yb_ref, rf_ref, rb_ref, g_ref,
                     d_ref, wglu_ref, bglu_ref, gn_ref, wout_ref, o_ref, *, head_dim):
    m = m_ref[0]
    uw = u_ref.shape[1]
    s5 = u_ref[...] * d_ref[...] + yf_ref[...] + yb_ref[...]
    ab = _dot(jax.nn.gelu(s5).astype(BF16), wglu_ref[...]) + bglu_ref[...]
    s5_o = ab[:, :uw] * _sigmoid(ab[:, uw:])
    ret = _group_norm(rf_ref[...] + rb_ref[...], head_dim) * gn_ref[...]
    ret_o = ret * _silu(g_ref[...])
    y = _dot(s5_o.astype(BF16), wout_ref[0:uw, :]) + _dot(ret_o.astype(BF16), wout_ref[uw:, :])
    o_ref[...] = x_ref[...] + m[2:3] * y


def _even_out(lay, stream, mods, p0, yf, yb, rf, rb, s5_d, w_glu, b_glu, gn_g, w_out, heads):
    rows, d = stream.shape
    uw = yf.shape[1]
    rw = rf.shape[1]
    tm = ROW_TILE
    gcol = (p0.shape[1] - rw) // rw

    def row(w, col=0):
        return pl.BlockSpec((tm, w), lambda i: (i, col))

    return pl.pallas_call(
        functools.partial(_even_out_kernel, head_dim=rw // heads),
        out_shape=jax.ShapeDtypeStruct((rows, d), F32),
        grid=(rows // tm,),
        in_specs=[row(d),
                  pl.BlockSpec((1,) + mods.shape[1:], lambda i: (lay.mod_row(i, tm), 0, 0)),
                  row(uw), row(uw), row(uw), row(rw), row(rw), row(rw, gcol),
                  _const_spec((1, uw)), _const_spec(w_glu.shape), _const_spec((1, 2 * uw)),
                  _const_spec((1, rw)), _const_spec(w_out.shape)],
        out_specs=row(d),
        compiler_params=_cparams(("arbitrary",)),
        name="even_out",
    )(stream, mods, p0, yf, yb, rf, rb, p0, s5_d.reshape(1, uw), w_glu,
      b_glu.reshape(1, 2 * uw), gn_g.reshape(1, rw), w_out)


def _ml_proj_kernel(x_ref, xp_ref, xn_ref, m_ref, g_ref, wx_ref, wo_ref, wg_ref, wgt_ref,
                    gb_ref, gbt_ref, cw_ref, cb_ref, wq_ref, wk_ref, wv_ref,
                    q_ref, k_ref, v_ref, op_ref, xc_ref, gc_ref, gr_ref, xm_scr,
                    *, heads, first_last, k_scale):
    i = pl.program_id(0)
    tm = x_ref.shape[0]
    halo = xp_ref.shape[0]
    m = m_ref[0]
    first, last = first_last(i)
    g = g_ref[...]
    h = _rms_mod(x_ref[...], g, m[0:1], m[1:2]).astype(BF16)
    hp = (_rms_mod(xp_ref[...], g, m[0:1], m[1:2]) * jnp.where(first, 0.0, 1.0)).astype(BF16)
    hn = (_rms_mod(xn_ref[...], g, m[0:1], m[1:2]) * jnp.where(last, 0.0, 1.0)).astype(BF16)

    op_ref[...] = _dot(h, wo_ref[...])

    gates_c = _dot(h, wg_ref[...]) + gb_ref[...]
    gates_r = _dot_nt(wgt_ref[...], h) + gbt_ref[...]
    ri = lax.broadcasted_iota(jnp.int32, (tm, tm), 0)
    ci = lax.broadcasted_iota(jnp.int32, (tm, tm), 1)
    same = (ri // CHUNK) == (ci // CHUNK)
    lower = jnp.where(jnp.logical_and(same, ci <= ri), 1.0, 0.0)
    upper = jnp.where(jnp.logical_and(same, ci >= ri), 1.0, 0.0)
    lane = lax.broadcasted_iota(jnp.int32, gates_c.shape, 1)
    kind_c = lane // heads
    lsc = _log_sigmoid(gates_c)
    cum_f = _dot_f32(lower, lsc)
    cum_b = _dot_f32(upper, lsc)
    gc_ref[...] = jnp.where(kind_c == 1, cum_f, jnp.where(kind_c == 3, cum_b, gates_c))
    sub = lax.broadcasted_iota(jnp.int32, gates_r.shape, 0)
    kind_r = sub // heads
    lsr = _log_sigmoid(gates_r)
    cum_fr = _dot_f32(lsr, upper)
    cum_br = _dot_f32(lsr, lower)
    gr_ref[...] = jnp.where(kind_r == 1, cum_fr, jnp.where(kind_r == 3, cum_br, gates_r))

    width = wx_ref.shape[1]
    cn = 512
    taps = cw_ref.shape[0]
    for c in range(width // cn):
        cs = slice(c * cn, (c + 1) * cn)
        w = wx_ref[:, cs]
        xm = _dot(h, w)
        xm_scr[0:halo, :] = _dot(hp, w)
        xm_scr[halo:halo + tm, :] = xm
        xm_scr[halo + tm:2 * halo + tm, :] = _dot(hn, w)
        acc = jnp.zeros((tm, cn), F32) + cb_ref[:, cs]
        for t in range(taps):
            off = halo + t - taps // 2
            acc = acc + xm_scr[off:off + tm, :] * cw_ref[t:t + 1, cs]
        xc = _silu(acc)
        xc_ref[:, cs] = xc
        xcb, xmb = xc.astype(BF16), xm.astype(BF16)
        bw = wq_ref.shape[1]
        for j in range(cn // bw):
            blk = c * (cn // bw) + j
            ls = slice(j * bw, (j + 1) * bw)
            os_ = slice(c * cn + j * bw, c * cn + (j + 1) * bw)
            q_ref[:, os_] = _dot(xcb[:, ls], wq_ref[blk]).astype(BF16)
            k_ref[:, os_] = (_dot(xcb[:, ls], wk_ref[blk]) * k_scale).astype(BF16)
            v_ref[:, os_] = _dot(xmb[:, ls], wv_ref[blk]).astype(BF16)


def _blockdiag_tiles(w, tile):
    nb, blk, _ = w.shape
    per = tile // blk
    wt = w.reshape(nb // per, per, blk, blk)
    eye = jnp.eye(per, dtype=bool)
    full = jnp.where(eye[None, :, None, :, None], wt[:, :, :, None, :], 0.0)
    return full.reshape(nb // per, tile, tile).astype(BF16)


def _ml_proj(lay, stream, mods, g, w_in, gate_b, conv_w, conv_b, wq, wk, wv, heads):
    rows, d = stream.shape
    inner = conv_w.shape[1]
    ng = 4 * heads
    tm = ML_ROW_TILE
    halo = CONV_HALO
    assert lay.ctx_len % tm == 0 and lay.seq % tm == 0 and tm % CHUNK == 0
    n_ctx = lay.ctx_rows // tm
    ctx_per, lat_per = lay.ctx_len // tm, lay.seq // tm

    def first_last(i):
        j = jnp.where(i < n_ctx, i % ctx_per, (i - n_ctx) % lat_per)
        per = jnp.where(i < n_ctx, ctx_per, lat_per)
        return j == 0, j == per - 1

    w_x = w_in[:, :inner].astype(BF16)
    w_o = w_in[:, inner:2 * inner].astype(BF16)
    w_g = w_in[:, 2 * inner:]
    w_gc = jnp.pad(w_g, ((0, 0), (0, 128 - ng))).astype(BF16)
    w_gt = w_g.T.astype(BF16)
    gb_c = jnp.pad(gate_b, (0, 128 - ng)).reshape(1, 128)
    gb_r = gate_b.reshape(ng, 1)
    tile = 256
    wq_t, wk_t, wv_t = (_blockdiag_tiles(w, tile) for w in (wq, wk, wv))
    r8 = tm // halo
    last8 = rows // halo - 1

    tok = lambda w, dt: jax.ShapeDtypeStruct((rows, w), dt)
    return pl.pallas_call(
        functools.partial(_ml_proj_kernel, heads=heads, first_last=first_last,
                          k_scale=(inner // heads) ** -0.5),
        out_shape=(tok(inner, BF16), tok(inner, BF16), tok(inner, BF16), tok(inner, F32),
                   tok(inner, F32), tok(128, F32), jax.ShapeDtypeStruct((ng, rows), F32)),
        grid=(rows // tm,),
        in_specs=[pl.BlockSpec((tm, d), lambda i: (i, 0)),
                  pl.BlockSpec((halo, d), lambda i: (jnp.maximum(i * r8 - 1, 0), 0)),
                  pl.BlockSpec((halo, d), lambda i: (jnp.minimum((i + 1) * r8, last8), 0)),
                  pl.BlockSpec((1,) + mods.shape[1:], lambda i: (lay.mod_row(i, tm), 0, 0)),
                  _const_spec((1, d)), _const_spec(w_x.shape), _const_spec(w_o.shape),
                  _const_spec(w_gc.shape), _const_spec(w_gt.shape), _const_spec(gb_c.shape),
                  _const_spec(gb_r.shape), _const_spec(conv_w.shape), _const_spec((1, inner)),
                  _const_spec(wq_t.shape), _const_spec(wk_t.shape), _const_spec(wv_t.shape)],
        out_specs=tuple([pl.BlockSpec((tm, inner), lambda i: (i, 0))] * 5
                        + [pl.BlockSpec((tm, 128), lambda i: (i, 0)),
                           pl.BlockSpec((ng, tm), lambda i: (0, i))]),
        scratch_shapes=[pltpu.VMEM((tm + 2 * halo, 512), F32)],
        compiler_params=_cparams(("arbitrary",)),
        name="mlstm_in_proj",
    )(stream, stream, stream, mods, g.reshape(1, d), w_x, w_o, w_gc, w_gt, gb_c, gb_r,
      conv_w, conv_b.reshape(1, inner), wq_t, wk_t, wv_t)


def _ml_scan_kernel(qf_ref, kf_ref, vf_ref, gcf_ref, grf_ref, qb_ref, kb_ref, vb_ref, gcb_ref,
                    grb_ref, hf_ref, hb_ref, c_ref, n_ref, m_ref, *, heads):
    i = pl.program_id(1)
    dh = qf_ref.shape[1] // heads

    @pl.when(i == 0)
    def _():
        c_ref[...] = jnp.zeros(c_ref.shape, F32)
        n_ref[...] = jnp.zeros(n_ref.shape, F32)
        m_ref[...] = jnp.zeros(m_ref.shape, F32)

    ii = lax.broadcasted_iota(jnp.int32, (CHUNK, CHUNK), 0)
    jj = lax.broadcasted_iota(jnp.int32, (CHUNK, CHUNK), 1)
    for d, (q_ref, k_ref, v_ref, gc_ref, gr_ref, h_ref) in enumerate(
            ((qf_ref, kf_ref, vf_ref, gcf_ref, grf_ref, hf_ref),
             (qb_ref, kb_ref, vb_ref, gcb_ref, grb_ref, hb_ref))):
        mask = (jj <= ii) if d == 0 else (jj >= ii)
        end = CHUNK - 1 if d == 0 else 0
        for h in range(heads):
            hs = slice(h * dh, (h + 1) * dh)
            ki, kb = (2 * d) * heads + h, (2 * d + 1) * heads + h
            q, k, v = q_ref[:, hs], k_ref[:, hs], v_ref[:, hs]
            ig_c = gc_ref[:, ki:ki + 1]
            b_c = gc_ref[:, kb:kb + 1]
            ig_r = gr_ref[ki:ki + 1, :]
            b_r = gr_ref[kb:kb + 1, :]
            m_prev = m_ref[d, h, 0:1, 0:1]
            log_w = jnp.where(mask, b_c - b_r + ig_r, -jnp.inf)
            log_prev = b_c + m_prev
            m_row = jnp.maximum(log_prev, jnp.max(log_w, axis=-1, keepdims=True))
            w = jnp.exp(log_w - m_row)
            w_prev = jnp.exp(log_prev - m_row)
            s = _dot_nt(q, k) * w
            cmat = c_ref[d, h]
            num = _dot(s.astype(BF16), v) + w_prev * _dot(q, cmat.astype(BF16))
            qn = jnp.sum(q.astype(F32) * n_ref[d, h, 0:1, :], axis=-1, keepdims=True)
            den = jnp.sum(s, axis=-1, keepdims=True) + w_prev * qn
            h_ref[:, hs] = num / jnp.maximum(jnp.abs(den), jnp.exp(-m_row))
            b_last = b_c[end:end + 1, :]
            log_k = b_last - b_c + ig_c
            m_new = jnp.maximum(b_last + m_prev, jnp.max(log_k, axis=0, keepdims=True))
            w_k = jnp.exp(log_k - m_new)
            w_c = jnp.exp(b_last + m_prev - m_new)
            kw = k.astype(F32) * w_k
            c_ref[d, h] = w_c * cmat + _dot_tn(kw.astype(BF16), v)
            n_ref[d, h, 0:1, :] = w_c * n_ref[d, h, 0:1, :] + jnp.sum(kw, axis=0, keepdims=True)
            m_ref[d, h] = jnp.broadcast_to(m_new, m_ref.shape[2:])


def _ml_scan(lay, q, k, v, gc, gr, heads):
    inner = q.shape[1]
    dh = inner // heads
    ncc, nlc = lay.ctx_len // CHUNK, lay.seq // CHUNK
    ng = gr.shape[0]
    ins = []
    for f in (lay.chunk_fwd, lay.chunk_bwd):
        ins += [pl.BlockSpec((CHUNK, inner), lambda b, i, f=f: (f(b, i), 0))] * 3
        ins += [pl.BlockSpec((CHUNK, 128), lambda b, i, f=f: (f(b, i), 0)),
                pl.BlockSpec((ng, CHUNK), lambda b, i, f=f: (0, f(b, i)))]
    outs = tuple(pl.BlockSpec((CHUNK, inner), lambda b, i, f=f: (f(b, i), 0))
                 for f in (lay.chunk_fwd, lay.chunk_bwd))
    return pl.pallas_call(
        functools.partial(_ml_scan_kernel, heads=heads),
        out_shape=(jax.ShapeDtypeStruct((lay.rows, inner), F32),) * 2,
        grid=(lay.batch, ncc + nlc),
        in_specs=ins,
        out_specs=outs,
        scratch_shapes=[pltpu.VMEM((2, heads, dh, dh), F32),
                        pltpu.VMEM((2, heads, 8, dh), F32),
                        pltpu.VMEM((2, heads, 8, 128), F32)],
        compiler_params=_cparams(("arbitrary", "arbitrary")),
        name="mlstm_scan",
    )(q, k, v, gc, gr, q, k, v, gc, gr)


def _ml_out_kernel(x_ref, m_ref, hf_ref, hb_ref, xc_ref, op_ref, gn_ref, sk_ref, wout_ref,
                   o_ref, *, head_dim):
    m = m_ref[0]
    hn = _group_norm(hf_ref[...] + hb_ref[...], head_dim) * gn_ref[...] + sk_ref[...] * xc_ref[...]
    y = _dot((_sigmoid(op_ref[...]) * hn).astype(BF16), wout_ref[...])
    o_ref[...] = x_ref[...] + m[2:3] * y


def _ml_out(lay, stream, mods, hf, hb, xc, o_pre, gn_g, skip, w_out, heads):
    rows, d = stream.shape
    inner = hf.shape[1]
    tm = ML_ROW_TILE
    sk = lay.ctx_rows // tm
    n_tiles = rows // tm - sk

    def row(w):
        return pl.BlockSpec((tm, w), lambda i: (i + sk, 0))

    return pl.pallas_call(
        functools.partial(_ml_out_kernel, head_dim=inner // heads),
        out_shape=jax.ShapeDtypeStruct((n_tiles * tm, d), F32),
        grid=(n_tiles,),
        in_specs=[row(d),
                  pl.BlockSpec((1,) + mods.shape[1:], lambda i: (lay.mod_row(i + sk, tm), 0, 0)),
                  row(inner), row(inner), row(inner), row(inner),
                  _const_spec((1, inner)), _const_spec((1, inner)), _const_spec(w_out.shape)],
        out_specs=pl.BlockSpec((tm, d), lambda i: (i, 0)),
        compiler_params=_cparams(("arbitrary",)),
        name="mlstm_out",
    )(stream, mods, hf, hb, xc, o_pre, gn_g.reshape(1, inner), skip.reshape(1, inner), w_out)


def kernel(x, c, ctx, c_ctx, mod_w, mod_b, norm_mix_g, norm_mlp_g, mlp_w1, mlp_w2, final_norm_g,
           ev_w_in, ev_w_out, s5_lambda_re, s5_lambda_im, s5_log_step, s5_b_re, s5_b_im, s5_c_re,
           s5_c_im, s5_d, s5_w_glu, s5_b_glu, ret_decay_logit, ret_gn_g,
           ml_w_in, ml_gate_b, ml_conv_w, ml_conv_b, ml_wq, ml_wk, ml_wv, ml_gn_g, ml_skip,
           ml_w_out):
    bsz, seq, d = x.shape
    ctx_len = ctx.shape[1]
    depth = mod_w.shape[0]
    assert depth == 2, "one even (S5 + retention) layer followed by one mLSTM layer"
    lay = _Layout(bsz, ctx_len, seq)
    assert lay.ctx_rows % ROW_TILE == 0 and seq % ROW_TILE == 0
    assert ctx_len % (S5_SEGS * S5_TB) == 0 and seq % (S5_SEGS * S5_TB) == 0
    assert ctx_len % CHUNK == 0 and seq % CHUNK == 0

    c_all = jnp.concatenate([c, c_ctx[None, :], jnp.zeros((8 - bsz - 1, d), F32)], axis=0)
    mods = _mod_table(c_all, mod_w, mod_b)
    stream = jnp.concatenate([ctx.reshape(bsz * ctx_len, d), x.reshape(bsz * seq, d)], axis=0)

    ret_w = ret_gn_g.shape[1]
    ret_heads = ret_decay_logit.shape[2]
    p0 = _proj(lay, stream, mods[0], norm_mix_g[0], ev_w_in[0].astype(BF16))
    yf, yb = _s5_mixer(lay, p0, s5_lambda_re[0], s5_lambda_im[0], s5_log_step[0],
                       s5_b_re[0], s5_b_im[0], s5_c_re[0], s5_c_im[0])
    rf, rb = _retention(lay, p0, ret_decay_logit[0], ret_w)
    stream = _even_out(lay, stream, mods[0], p0, yf, yb, rf, rb, s5_d[0],
                       s5_w_glu[0].astype(BF16), s5_b_glu[0], ret_gn_g[0],
                       ev_w_out[0].astype(BF16), ret_heads)
    stream = _mlp(lay, stream, mods[0], norm_mlp_g[0], mlp_w1[0].astype(BF16),
                  mlp_w2[0].astype(BF16), final_norm_g, latent_only=False, final=False)

    ml_heads = ml_gate_b.shape[1] // 4
    q, k, v, o_pre, xc, gc, gr = _ml_proj(lay, stream, mods[1], norm_mix_g[1], ml_w_in[0],
                                          ml_gate_b[0], ml_conv_w[0], ml_conv_b[0],
                                          ml_wq[0], ml_wk[0], ml_wv[0], ml_heads)
    hf, hb = _ml_scan(lay, q, k, v, gc, gr, ml_heads)
    lat = _ml_out(lay, stream, mods[1], hf, hb, xc, o_pre, ml_gn_g[0], ml_skip[0],
                  ml_w_out[0].astype(BF16), ml_heads)
    lay_lat = _Layout(bsz, 0, seq)
    out = _mlp(lay_lat, lat, mods[1], norm_mlp_g[1], mlp_w1[1].astype(BF16),
               mlp_w2[1].astype(BF16), final_norm_g, latent_only=False, final=True)
    return out.reshape(bsz, seq, d)
```

```python
import functools
import math

import jax
import jax.numpy as jnp
from jax import lax
from jax.experimental import pallas as pl
from jax.experimental.pallas import tpu as pltpu

F32 = jnp.float32
BF16 = jnp.bfloat16

EPS = 1e-6
GRID_W = 64
ROPE_BASE = 10000.0
CHUNK = 128
ML_QKV_BLOCK = 4
V7X_SCOPED_VMEM_BYTES = 60000 * 1024
S5_SEGS = 4
S5_TB = 32
S5_PAIRS_PER_PASS = 4
ROW_TILE = 512
ML_ROW_TILE = 256
CONV_HALO = 8


def _cparams(sem, vmem_bytes=V7X_SCOPED_VMEM_BYTES, flags=None):
    return pltpu.CompilerParams(dimension_semantics=sem, vmem_limit_bytes=vmem_bytes, flags=flags)


def _const_spec(shape):
    nd = len(shape)
    return pl.BlockSpec(shape, lambda *_: (0,) * nd, pipeline_mode=pl.Buffered(1))


def _dot(a, b):
    return jnp.dot(a, b, preferred_element_type=F32)


def _dot_nt(a, b):
    return lax.dot_general(a, b, (((1,), (1,)), ((), ())), preferred_element_type=F32)


def _dot_tn(a, b):
    return lax.dot_general(a, b, (((0,), (0,)), ((), ())), preferred_element_type=F32)


def _dot_f32(a, b):
    return jnp.dot(a, b, preferred_element_type=F32, precision=lax.Precision.HIGHEST)


def _sigmoid(x):
    return jax.nn.sigmoid(x)


def _silu(x):
    return x * jax.nn.sigmoid(x)


def _log_sigmoid(x):
    return jnp.minimum(x, 0.0) - jnp.log1p(jnp.exp(-jnp.abs(x)))


def _rms_mod(x, g, shift, scale):
    y = x * lax.rsqrt(jnp.mean(x * x, axis=-1, keepdims=True) + EPS) * g
    return y * (1.0 + scale) + shift


def _group_norm(x, width):
    outs = []
    for h in range(x.shape[-1] // width):
        xh = x[:, h * width:(h + 1) * width]
        xc = xh - jnp.mean(xh, axis=-1, keepdims=True)
        var = jnp.mean(xc * xc, axis=-1, keepdims=True)
        outs.append(xc * lax.rsqrt(var + EPS))
    return jnp.concatenate(outs, axis=-1)


class _Layout:
    def __init__(self, batch, ctx_len, seq):
        self.batch, self.ctx_len, self.seq = batch, ctx_len, seq
        self.ctx_rows = batch * ctx_len
        self.rows = batch * (ctx_len + seq)

    def mod_row(self, tile, tile_rows):
        n_ctx = self.ctx_rows // tile_rows
        return jnp.where(tile < n_ctx, self.batch, (tile - n_ctx) // (self.seq // tile_rows))

    def chunk_fwd(self, b, i):
        ncc, nlc = self.ctx_len // CHUNK, self.seq // CHUNK
        return jnp.where(i < ncc, b * ncc + i, self.batch * ncc + b * nlc + (i - ncc))

    def chunk_bwd(self, b, i):
        ncc, nlc = self.ctx_len // CHUNK, self.seq // CHUNK
        return jnp.where(i < ncc, b * ncc + (ncc - 1 - i),
                         self.batch * ncc + b * nlc + (nlc - 1 - (i - ncc)))


def _mod_kernel(c_ref, w_ref, b_ref, o_ref):
    s = _silu(c_ref[...]).astype(BF16)
    o_ref[0] = _dot(s, w_ref[0].astype(BF16)) + b_ref[0]


def _mod_table(c_all, mod_w, mod_b):
    depth, d, n = mod_w.shape
    tn = 1536
    out = pl.pallas_call(
        _mod_kernel,
        out_shape=jax.ShapeDtypeStruct((depth, c_all.shape[0], n), F32),
        grid=(depth, n // tn),
        in_specs=[pl.BlockSpec(c_all.shape, lambda l, j: (0, 0)),
                  pl.BlockSpec((1, d, tn), lambda l, j: (l, 0, j)),
                  pl.BlockSpec((1, 1, tn), lambda l, j: (l, 0, j))],
        out_specs=pl.BlockSpec((1, c_all.shape[0], tn), lambda l, j: (l, 0, j)),
        compiler_params=_cparams(("arbitrary", "arbitrary")),
        name="mod_table",
    )(c_all, mod_w, mod_b.reshape(depth, 1, n))
    return out.reshape(depth, c_all.shape[0], n // d, d)


def _proj_kernel(x_ref, m_ref, g_ref, w_ref, o_ref, *, tn):
    m = m_ref[0]
    h = _rms_mod(x_ref[...], g_ref[...], m[0:1], m[1:2]).astype(BF16)
    for c in range(o_ref.shape[1] // tn):
        o_ref[:, c * tn:(c + 1) * tn] = _dot(h, w_ref[:, c * tn:(c + 1) * tn])


def _proj(lay, stream, mods, g, w):
    rows, d = stream.shape
    n = w.shape[1]
    tm = ROW_TILE
    return pl.pallas_call(
        functools.partial(_proj_kernel, tn=512),
        out_shape=jax.ShapeDtypeStruct((rows, n), F32),
        grid=(rows // tm,),
        in_specs=[pl.BlockSpec((tm, d), lambda i: (i, 0)),
                  pl.BlockSpec((1,) + mods.shape[1:], lambda i: (lay.mod_row(i, tm), 0, 0)),
                  _const_spec((1, d)),
                  _const_spec(w.shape)],
        out_specs=pl.BlockSpec((tm, n), lambda i: (i, 0)),
        compiler_params=_cparams(("arbitrary",)),
        name="even_in_proj",
    )(stream, mods, g.reshape(1, d), w)


def _mlp_kernel(x_ref, m_ref, g_ref, w1_ref, w2_ref, fg_ref, o_ref, *, hc, final):
    x = x_ref[...]
    m = m_ref[0]
    h = _rms_mod(x, g_ref[...], m[3:4], m[4:5]).astype(BF16)
    acc = jnp.zeros(x.shape, F32)
    for c in range(w1_ref.shape[1] // hc):
        a = _dot(h, w1_ref[:, c * hc:(c + 1) * hc])
        a = jnp.square(jnp.maximum(a, 0.0)).astype(BF16)
        acc = acc + _dot(a, w2_ref[c * hc:(c + 1) * hc, :])
    y = x + m[5:6] * acc
    if final:
        y = y * lax.rsqrt(jnp.mean(y * y, axis=-1, keepdims=True) + EPS) * fg_ref[...]
    o_ref[...] = y


def _mlp(lay, stream, mods, g, w1, w2, final_g, *, latent_only, final):
    rows, d = stream.shape
    tm = ROW_TILE
    skip = lay.ctx_rows // tm if latent_only else 0
    n_tiles = rows // tm - skip
    return pl.pallas_call(
        functools.partial(_mlp_kernel, hc=512, final=final),
        out_shape=jax.ShapeDtypeStruct((n_tiles * tm, d), F32),
        grid=(n_tiles,),
        in_specs=[pl.BlockSpec((tm, d), lambda i: (i + skip, 0)),
                  pl.BlockSpec((1,) + mods.shape[1:], lambda i: (lay.mod_row(i + skip, tm), 0, 0)),
                  _const_spec((1, d)),
                  _const_spec(w1.shape),
                  _const_spec(w2.shape),
                  _const_spec((1, d))],
        out_specs=pl.BlockSpec((tm, d), lambda i: (i, 0)),
        compiler_params=_cparams(("arbitrary",)),
        name="mlp_final" if final else "mlp",
    )(stream, mods, g.reshape(1, d), w1, w2, final_g.reshape(1, d))


def _s5_disc_kernel(lre_ref, lim_ref, ls_ref, bre_ref, bim_ref,
                    lam_ref, bbar_ref, pow_ref, *, seg_ctx, seg_lat):
    lre, lim = lre_ref[...], lim_ref[...]
    step = jnp.exp(ls_ref[...])
    mag = jnp.exp(lre * step)
    ang = lim * step
    lbr, lbi = mag * jnp.cos(ang), mag * jnp.sin(ang)
    lam_ref[0], lam_ref[1] = lbr, lbi
    den = lre * lre + lim * lim
    nr, ni = lbr - 1.0, lbi
    cr = (nr * lre + ni * lim) / den
    ci = (ni * lre - nr * lim) / den
    bre, bim = bre_ref[...], bim_ref[...]
    bbar_ref[0] = cr[:, None] * bre - ci[:, None] * bim
    bbar_ref[1] = cr[:, None] * bim + ci[:, None] * bre
    for k, n in enumerate((seg_ctx, seg_lat)):
        mag_n = jnp.exp(lre * step * n)
        ang_n = lim * step * n
        pow_ref[2 * k] = mag_n * jnp.cos(ang_n)
        pow_ref[2 * k + 1] = mag_n * jnp.sin(ang_n)


def _s5_discretize(lam_re, lam_im, log_step, b_re, b_im, seg_ctx, seg_lat):
    _, g, p = lam_re.shape
    n = b_re.shape[-1]
    gp = g * p
    lt = gp // 128
    shp = (2, lt, 128)
    ls = jnp.broadcast_to(log_step[:, :, None], (2, g, p)).reshape(shp)
    bre = b_re.reshape(2, gp, n).transpose(0, 2, 1).reshape(2, n, lt, 128)
    bim = b_im.reshape(2, gp, n).transpose(0, 2, 1).reshape(2, n, lt, 128)
    lam, bbar, pw = pl.pallas_call(
        functools.partial(_s5_disc_kernel, seg_ctx=seg_ctx, seg_lat=seg_lat),
        out_shape=(jax.ShapeDtypeStruct((2,) + shp, F32),
                   jax.ShapeDtypeStruct((2, 2, n, lt, 128), F32),
                   jax.ShapeDtypeStruct((4,) + shp, F32)),
        name="s5_discretize",
    )(lam_re.reshape(shp), lam_im.reshape(shp), ls, bre, bim)
    return lam.reshape(2, 2, gp), bbar.reshape(2, 2, n, gp), pw.reshape(4, 2, gp)


def _s5_scan_kernel(*refs, tb, ncb, nblk, half, final_pass):
    if final_pass:
        (uf_ref, ub_ref, wb_ref, wc_ref, lam_ref, pow_ref, fin_ref,
         yf_ref, yb_ref, bu_f, bu_b, st_ref, init_ref) = refs
    else:
        uf_ref, ub_ref, wb_ref, lam_ref, fin_ref, bu_f, bu_b, st_ref = refs
    i = pl.program_id(0)
    part = jnp.where(i < ncb, 0, 1)
    first = jnp.logical_or(i == 0, i == ncb)
    last = jnp.logical_or(i == ncb - 1, i == nblk - 1)
    nseg = S5_SEGS
    nrow = st_ref.shape[1]

    if final_pass:
        @pl.when(i == 0)
        def _():
            for d in range(2):
                for b in range(nrow // nseg):
                    cr = jnp.zeros((1, half), F32)
                    ci = jnp.zeros((1, half), F32)
                    for prt in range(2):
                        pr = pow_ref[2 * prt, d:d + 1, :]
                        pi = pow_ref[2 * prt + 1, d:d + 1, :]
                        for s in range(nseg):
                            seg = s if d == 0 else nseg - 1 - s
                            r = b * nseg + seg
                            init_ref[d, prt, r:r + 1, 0:half] = cr
                            init_ref[d, prt, r:r + 1, half:2 * half] = ci
                            fr = fin_ref[d, prt, r:r + 1, 0:half]
                            fi = fin_ref[d, prt, r:r + 1, half:2 * half]
                            cr, ci = pr * cr - pi * ci + fr, pr * ci + pi * cr + fi

    @pl.when(first)
    def _():
        if final_pass:
            for d in range(2):
                st_ref[d] = jnp.where(part == 0, init_ref[d, 0], init_ref[d, 1])
        else:
            st_ref[...] = jnp.zeros(st_ref.shape, F32)

    kt_n = uf_ref.shape[1] // 128
    wcol = wb_ref.shape[3] // 2
    for d, (u_ref, bu_ref) in enumerate(((uf_ref, bu_f), (ub_ref, bu_b))):
        for kt in range(kt_n):
            r = _dot(u_ref[:, kt * 128:(kt + 1) * 128], wb_ref[d, kt])
            bu_ref[:, kt * wcol:(kt + 1) * wcol] = r[:, :wcol]
            bu_ref[:, half + kt * wcol:half + (kt + 1) * wcol] = r[:, wcol:]

    npair = half // 128
    ppp = S5_PAIRS_PER_PASS
    for d, bu_ref in enumerate((bu_f, bu_b)):
        for pb in range(npair // ppp):
            cols = [(pb * ppp + j) * 128 for j in range(ppp)]
            lr = [jnp.broadcast_to(lam_ref[0, d:d + 1, c:c + 128], (nrow, 128)) for c in cols]
            li = [jnp.broadcast_to(lam_ref[1, d:d + 1, c:c + 128], (nrow, 128)) for c in cols]
            hr0 = tuple(st_ref[d, :, c:c + 128] for c in cols)
            hi0 = tuple(st_ref[d, :, half + c:half + c + 128] for c in cols)

            def body(t, carry, d=d, bu_ref=bu_ref, cols=cols, lr=lr, li=li):
                hr, hi = carry
                tt = t if d == 0 else tb - 1 - t
                r0 = pl.multiple_of(tt * nrow, nrow)
                nhr, nhi = [], []
                for j, c in enumerate(cols):
                    br = bu_ref[pl.ds(r0, nrow), c:c + 128]
                    bi = bu_ref[pl.ds(r0, nrow), half + c:half + c + 128]
                    xr = lr[j] * hr[j] - li[j] * hi[j] + br
                    xi = lr[j] * hi[j] + li[j] * hr[j] + bi
                    if final_pass:
                        bu_ref[pl.ds(r0, nrow), c:c + 128] = xr
                        bu_ref[pl.ds(r0, nrow), half + c:half + c + 128] = xi
                    nhr.append(xr)
                    nhi.append(xi)
                return tuple(nhr), tuple(nhi)

            hr, hi = lax.fori_loop(0, tb, body, (hr0, hi0), unroll=4)
            for j, c in enumerate(cols):
                st_ref[d, :, c:c + 128] = hr[j]
                st_ref[d, :, half + c:half + c + 128] = hi[j]

    if final_pass:
        nt_n = wc_ref.shape[1]
        ncol = wc_ref.shape[3]
        krow = wc_ref.shape[2] // 2
        for d, (bu_ref, y_ref) in enumerate(((bu_f, yf_ref), (bu_b, yb_ref))):
            for nt in range(nt_n):
                h_re = bu_ref[:, nt * krow:(nt + 1) * krow].astype(BF16)
                h_im = bu_ref[:, half + nt * krow:half + (nt + 1) * krow].astype(BF16)
                y_ref[:, nt * ncol:(nt + 1) * ncol] = (
                    _dot(h_re, wc_ref[d, nt, 0:krow, :]) - _dot(h_im, wc_ref[d, nt, krow:2 * krow, :]))
    else:
        @pl.when(last)
        def _():
            for d in range(2):
                for prt in range(2):
                    @pl.when(part == prt)
                    def _(d=d, prt=prt):
                        fin_ref[d, prt] = st_ref[d]


def _s5_scan(u_tm, wb, wc, lam, pw, fin, *, ncb, final_pass):
    nrow = fin.shape[2]
    half = fin.shape[3] // 2
    tb = S5_TB
    blk = tb * nrow
    nblk = u_tm.shape[0] // blk
    uw = u_tm.shape[1]

    def f_idx(i):
        return (i, 0)

    def b_idx(i):
        return (jnp.where(i < ncb, ncb - 1 - i, nblk - 1 - (i - ncb)), 0)

    kern = functools.partial(_s5_scan_kernel, tb=tb, ncb=ncb, nblk=nblk, half=half,
                             final_pass=final_pass)
    scratch = [pltpu.VMEM((blk, 2 * half), F32), pltpu.VMEM((blk, 2 * half), F32),
               pltpu.VMEM((2, nrow, 2 * half), F32)]
    u_specs = [pl.BlockSpec((blk, uw), f_idx), pl.BlockSpec((blk, uw), b_idx)]
    if final_pass:
        return pl.pallas_call(
            kern,
            out_shape=(jax.ShapeDtypeStruct((u_tm.shape[0], wc.shape[1] * wc.shape[3]), F32),) * 2,
            grid=(nblk,),
            in_specs=u_specs + [_const_spec(wb.shape), _const_spec(wc.shape), _const_spec(lam.shape),
                                _const_spec(pw.shape), _const_spec(fin.shape)],
            out_specs=(pl.BlockSpec((blk, wc.shape[1] * wc.shape[3]), f_idx),
                       pl.BlockSpec((blk, wc.shape[1] * wc.shape[3]), b_idx)),
            scratch_shapes=scratch + [pltpu.VMEM(fin.shape, F32)],
            compiler_params=_cparams(("arbitrary",)),
            name="s5_scan_out",
        )(u_tm, u_tm, wb, wc, lam, pw, fin)
    return pl.pallas_call(
        kern,
        out_shape=jax.ShapeDtypeStruct(fin.shape, F32),
        grid=(nblk,),
        in_specs=u_specs + [_const_spec(wb.shape), _const_spec(lam.shape)],
        out_specs=pl.BlockSpec(fin.shape, lambda i: (0, 0, 0, 0)),
        scratch_shapes=scratch,
        compiler_params=_cparams(("arbitrary",)),
        name="s5_scan_state",
    )(u_tm, u_tm, wb, lam)


def _s5_mixer(lay, p0, lam_re, lam_im, log_step, b_re, b_im, c_re, c_im):
    bsz = lay.batch
    _, g, pst = lam_re.shape
    ngrp = b_re.shape[-1]
    uw = g * ngrp
    half = g * pst
    nseg = S5_SEGS
    seg_ctx, seg_lat = lay.ctx_len // nseg, lay.seq // nseg
    nrow = bsz * nseg

    lam, bbar, pw = _s5_discretize(lam_re, lam_im, log_step, b_re, b_im, seg_ctx, seg_lat)

    kt_g = 128 // ngrp
    kt_n = g // kt_g
    bb = bbar.reshape(2, 2, ngrp, kt_n, kt_g, pst)
    eye = jnp.eye(kt_g, dtype=bool)
    wb = jnp.where(eye[None, None, None, :, None, :, None],
                   bb.transpose(0, 1, 3, 2, 4, 5)[:, :, :, None, :, :, :], 0.0)
    wb = wb.reshape(2, 2, kt_n, kt_g * ngrp, kt_g * pst)
    wb = jnp.concatenate([wb[0], wb[1]], axis=-1).astype(BF16)

    nt_g = 256 // ngrp
    nt_n = g // nt_g
    eye_c = jnp.eye(nt_g, dtype=bool)

    def c_tiles(c):
        cc = c.reshape(2, nt_n, nt_g, ngrp, pst).transpose(0, 1, 2, 4, 3)
        w = jnp.where(eye_c[None, None, :, None, :, None], cc[:, :, :, :, None, :], 0.0)
        return w.reshape(2, nt_n, nt_g * pst, nt_g * ngrp)

    wc = jnp.concatenate([c_tiles(c_re), c_tiles(c_im)], axis=2).astype(BF16)

    u = p0[:, :uw].astype(BF16)
    u_c = u[:lay.ctx_rows].reshape(nrow, seg_ctx, uw).transpose(1, 0, 2)
    u_l = u[lay.ctx_rows:].reshape(nrow, seg_lat, uw).transpose(1, 0, 2)
    u_tm = jnp.concatenate([u_c, u_l], axis=0).reshape((seg_ctx + seg_lat) * nrow, uw)

    ncb = seg_ctx // S5_TB
    fin0 = jnp.zeros((2, 2, nrow, 2 * half), F32)
    fin = _s5_scan(u_tm, wb, None, lam, None, fin0, ncb=ncb, final_pass=False)
    yf, yb = _s5_scan(u_tm, wb, wc, lam, pw, fin, ncb=ncb, final_pass=True)

    def natural(y):
        y = y.reshape(seg_ctx + seg_lat, nrow, uw)
        y_c = y[:seg_ctx].transpose(1, 0, 2).reshape(lay.ctx_rows, uw)
        y_l = y[seg_ctx:].transpose(1, 0, 2).reshape(bsz * lay.seq, uw)
        return jnp.concatenate([y_c, y_l], axis=0)

    return natural(yf), natural(yb)


def _rope_tables(chunk_idx, col_cos, col_sin, is_lat):
    lane = lax.broadcasted_iota(jnp.int32, (8, 128), 1)
    sub = lax.broadcasted_iota(jnp.int32, (8, 128), 0)
    quarter = 32
    inv = jnp.exp((lane & (quarter - 1)).astype(F32) * (-math.log(ROPE_BASE) / quarter))
    rows_per_chunk = CHUNK // GRID_W
    ang = (chunk_idx * rows_per_chunk + sub).astype(F32) * inv
    rc, rs = jnp.cos(ang), jnp.sin(ang)
    tok = lax.broadcasted_iota(jnp.int32, (CHUNK, 128), 0)
    lane_t = lax.broadcasted_iota(jnp.int32, (CHUNK, 128), 1)
    row_cos = jnp.zeros((CHUNK, 128), F32)
    row_sin = jnp.zeros((CHUNK, 128), F32)
    for r in range(rows_per_chunk):
        sel = (tok // GRID_W) == r
        row_cos = jnp.where(sel, rc[r:r + 1, :], row_cos)
        row_sin = jnp.where(sel, rs[r:r + 1, :], row_sin)
    is_col = (lane_t & quarter) != 0
    cos_t = jnp.where(is_col, col_cos, row_cos)
    sin_t = jnp.where(is_col, col_sin, row_sin)
    sin_t = jnp.where(lane_t < 64, -sin_t, sin_t)
    cos_t = jnp.where(is_lat, cos_t, 1.0)
    sin_t = jnp.where(is_lat, sin_t, 0.0)
    return cos_t, sin_t


def _ret_kernel(dl_ref, qf_ref, kf_ref, vf_ref, qb_ref, kb_ref, vb_ref, of_ref, ob_ref,
                st_ref, tab_ref, col_ref, *, heads, ncc, nlc):
    i = pl.program_id(1)
    dh = qf_ref.shape[1] // heads
    ii = lax.broadcasted_iota(jnp.int32, (CHUNK, CHUNK), 0)
    jj = lax.broadcasted_iota(jnp.int32, (CHUNK, CHUNK), 1)

    @pl.when(i == 0)
    def _():
        st_ref[...] = jnp.zeros(st_ref.shape, F32)
        relf = (ii - jj).astype(F32)
        rowf = ii.astype(F32)
        for d in range(2):
            for h in range(heads):
                lg = _log_sigmoid(jnp.full((CHUNK, CHUNK), dl_ref[d, h], F32))
                if d == 0:
                    mask = ii >= jj
                    intra = jnp.where(mask, jnp.exp(jnp.where(mask, relf, 0.0) * lg), 0.0)
                    dq = jnp.exp((rowf + 1.0) * lg)
                    dk = jnp.exp((CHUNK - 1.0 - rowf) * lg)
                else:
                    mask = jj > ii
                    intra = jnp.where(mask, jnp.exp(jnp.where(mask, -relf, 0.0) * lg), 0.0)
                    dq = jnp.exp((CHUNK - rowf) * lg)
                    dk = jnp.exp(rowf * lg)
                tab_ref[d, h, 0] = intra
                tab_ref[d, h, 1] = dq
                tab_ref[d, h, 2] = dk
                tab_ref[d, h, 3] = jnp.exp(CHUNK * lg)
        lane = lax.broadcasted_iota(jnp.int32, (CHUNK, 128), 1)
        tok = lax.broadcasted_iota(jnp.int32, (CHUNK, 128), 0)
        inv = jnp.exp((lane & 31).astype(F32) * (-math.log(ROPE_BASE) / 32))
        ang = (tok % GRID_W).astype(F32) * inv
        col_ref[0] = jnp.cos(ang)
        col_ref[1] = jnp.sin(ang)

    is_lat = i >= ncc
    scale = dh ** -0.5
    for d, (q_ref, k_ref, v_ref, o_ref) in enumerate(((qf_ref, kf_ref, vf_ref, of_ref),
                                                      (qb_ref, kb_ref, vb_ref, ob_ref))):
        cidx = (i - ncc) if d == 0 else nlc - 1 - (i - ncc)
        cos_t, sin_t = _rope_tables(jnp.maximum(cidx, 0), col_ref[0], col_ref[1], is_lat)
        for h in range(heads):
            q = q_ref[:, h * dh:(h + 1) * dh]
            k = k_ref[:, h * dh:(h + 1) * dh]
            v = v_ref[:, h * dh:(h + 1) * dh].astype(BF16)
            q = q * cos_t + pltpu.roll(q, dh // 2, 1) * sin_t
            k = (k * cos_t + pltpu.roll(k, dh // 2, 1) * sin_t) * scale
            qb = q.astype(BF16)
            state = st_ref[d, h]
            s = _dot_nt(qb, k.astype(BF16)) * tab_ref[d, h, 0]
            o = _dot(s.astype(BF16), v) + _dot(qb, state.astype(BF16)) * tab_ref[d, h, 1]
            o_ref[:, h * dh:(h + 1) * dh] = o
            kd = (k * tab_ref[d, h, 2]).astype(BF16)
            st_ref[d, h] = state * tab_ref[d, h, 3] + _dot_tn(kd, v)


def _retention(lay, p0, decay_logit, width):
    heads = decay_logit.shape[1]
    dh = width // heads
    assert dh == 128 and CHUNK % GRID_W == 0
    ncc, nlc = lay.ctx_len // CHUNK, lay.seq // CHUNK
    ucols = (p0.shape[1] - 4 * width) // width

    def spec(col, idx):
        return pl.BlockSpec((CHUNK, width), lambda b, i: (idx(b, i), col))

    ins = [spec(ucols + c, f) for f in (lay.chunk_fwd, lay.chunk_bwd) for c in range(3)]
    outs = tuple(pl.BlockSpec((CHUNK, width), lambda b, i, f=f: (f(b, i), 0))
                 for f in (lay.chunk_fwd, lay.chunk_bwd))
    return pl.pallas_call(
        functools.partial(_ret_kernel, heads=heads, ncc=ncc, nlc=nlc),
        out_shape=(jax.ShapeDtypeStruct((lay.rows, width), F32),) * 2,
        grid=(lay.batch, ncc + nlc),
        in_specs=[pl.BlockSpec(memory_space=pltpu.SMEM)] + ins,
        out_specs=outs,
        scratch_shapes=[pltpu.VMEM((2, heads, dh, dh), F32),
                        pltpu.VMEM((2, heads, 4, CHUNK, CHUNK), F32),
                        pltpu.VMEM((2, CHUNK, 128), F32)],
        compiler_params=_cparams(("arbitrary", "arbitrary")),
        name="retention",
    )(decay_logit, p0, p0, p0, p0, p0, p0)


def _even_out_kernel(x_ref, m_ref, u_ref, yf_ref, yb_ref, rf_ref, rb_ref, g_ref,
                     d_ref, wglu_ref, bglu_ref, gn_ref, wout_ref, o_ref, *, head_dim):
    m = m_ref[0]
    uw = u_ref.shape[1]
    s5 = u_ref[...] * d_ref[...] + yf_ref[...] + yb_ref[...]
    ab = _dot(jax.nn.gelu(s5).astype(BF16), wglu_ref[...]) + bglu_ref[...]
    s5_o = ab[:, :uw] * _sigmoid(ab[:, uw:])
    ret = _group_norm(rf_ref[...] + rb_ref[...], head_dim) * gn_ref[...]
    ret_o = ret * _silu(g_ref[...])
    y = _dot(s5_o.astype(BF16), wout_ref[0:uw, :]) + _dot(ret_o.astype(BF16), wout_ref[uw:, :])
    o_ref[...] = x_ref[...] + m[2:3] * y


def _even_out(lay, stream, mods, p0, yf, yb, rf, rb, s5_d, w_glu, b_glu, gn_g, w_out, heads):
    rows, d = stream.shape
    uw = yf.shape[1]
    rw = rf.shape[1]
    tm = ROW_TILE
    gcol = (p0.shape[1] - rw) // rw

    def row(w, col=0):
        return pl.BlockSpec((tm, w), lambda i: (i, col))

    return pl.pallas_call(
        functools.partial(_even_out_kernel, head_dim=rw // heads),
        out_shape=jax.ShapeDtypeStruct((rows, d), F32),
        grid=(rows // tm,),
        in_specs=[row(d),
                  pl.BlockSpec((1,) + mods.shape[1:], lambda i: (lay.mod_row(i, tm), 0, 0)),
                  row(uw), row(uw), row(uw), row(rw), row(rw), row(rw, gcol),
                  _const_spec((1, uw)), _const_spec(w_glu.shape), _const_spec((1, 2 * uw)),
                  _const_spec((1, rw)), _const_spec(w_out.shape)],
        out_specs=row(d),
        compiler_params=_cparams(("arbitrary",)),
        name="even_out",
    )(stream, mods, p0, yf, yb, rf, rb, p0, s5_d.reshape(1, uw), w_glu,
      b_glu.reshape(1, 2 * uw), gn_g.reshape(1, rw), w_out)


def _ml_proj_kernel(x_ref, xp_ref, xn_ref, m_ref, g_ref, wx_ref, wo_ref, wgt_ref,
                    gbt_ref, cw_ref, cb_ref, wq_ref, wk_ref, wv_ref,
                    q_ref, k_ref, v_ref, op_ref, xc_ref, gc_ref, gr_ref, xm_scr,
                    *, heads, first_last, k_scale):
    i = pl.program_id(0)
    tm = x_ref.shape[0]
    halo = xp_ref.shape[0]
    m = m_ref[0]
    first, last = first_last(i)
    g = g_ref[...]
    hf = _rms_mod(x_ref[...], g, m[0:1], m[1:2])
    hp = _rms_mod(xp_ref[...], g, m[0:1], m[1:2]) * jnp.where(first, 0.0, 1.0)
    hn = _rms_mod(xn_ref[...], g, m[0:1], m[1:2]) * jnp.where(last, 0.0, 1.0)
    h = hf.astype(BF16)
    h_ext = jnp.concatenate([hp, hf, hn], axis=0).astype(BF16)

    op_ref[...] = _dot(h, wo_ref[...]).astype(op_ref.dtype)

    gates = _dot_nt(wgt_ref[...], h) + gbt_ref[...]
    si = lax.broadcasted_iota(jnp.int32, (tm, tm), 0)
    ti = lax.broadcasted_iota(jnp.int32, (tm, tm), 1)
    same = (si // CHUNK) == (ti // CHUNK)
    upto = jnp.where(jnp.logical_and(same, si <= ti), 1.0, 0.0)
    from_ = jnp.where(jnp.logical_and(same, si >= ti), 1.0, 0.0)
    kind = lax.broadcasted_iota(jnp.int32, gates.shape, 0) // heads
    ls = _log_sigmoid(gates)
    cum_f = _dot_f32(ls, upto)
    cum_b = _dot_f32(ls, from_)
    gr = jnp.where(kind == 1, cum_f, jnp.where(kind == 3, cum_b, gates))
    gr_ref[...] = gr
    pad = jnp.zeros((gc_ref.shape[1] - gr.shape[0], tm), F32)
    gc_ref[...] = jnp.concatenate([gr, pad], axis=0).T

    width = wx_ref.shape[1]
    cn = 512
    taps = cw_ref.shape[0]
    for c in range(width // cn):
        cs = slice(c * cn, (c + 1) * cn)
        xm_scr[c] = _dot(h_ext, wx_ref[:, cs])
        xm = xm_scr[c, halo:halo + tm, :]
        acc = jnp.zeros((tm, cn), F32) + cb_ref[:, cs]
        for t in range(taps):
            off = halo + t - taps // 2
            acc = acc + xm_scr[c, off:off + tm, :] * cw_ref[t:t + 1, cs]
        xc = _silu(acc)
        xc_ref[:, cs] = xc.astype(xc_ref.dtype)
        xcb, xmb = xc.astype(BF16), xm.astype(BF16)
        bw = wq_ref.shape[1]
        for j in range(cn // bw):
            blk = c * (cn // bw) + j
            ls_ = slice(j * bw, (j + 1) * bw)
            os_ = slice(c * cn + j * bw, c * cn + (j + 1) * bw)
            q_ref[:, os_] = _dot(xcb[:, ls_], wq_ref[blk]).astype(BF16)
            k_ref[os_, :] = (_dot_nt(wk_ref[blk], xcb[:, ls_]) * k_scale).astype(BF16)
            v_ref[:, os_] = _dot(xmb[:, ls_], wv_ref[blk]).astype(BF16)


def _blockdiag_tiles(w, tile):
    nb, blk, _ = w.shape
    per = tile // blk
    wt = w.reshape(nb // per, per, blk, blk)
    eye = jnp.eye(per, dtype=bool)
    full = jnp.where(eye[None, :, None, :, None], wt[:, :, :, None, :], 0.0)
    return full.reshape(nb // per, tile, tile).astype(BF16)


def _ml_proj(lay, stream, mods, g, w_in, gate_b, conv_w, conv_b, wq, wk, wv, heads):
    rows, d = stream.shape
    inner = conv_w.shape[1]
    ng = 4 * heads
    tm = ML_ROW_TILE
    halo = CONV_HALO
    assert lay.ctx_len % tm == 0 and lay.seq % tm == 0 and tm % CHUNK == 0
    n_ctx = lay.ctx_rows // tm
    ctx_per, lat_per = lay.ctx_len // tm, lay.seq // tm

    def first_last(i):
        j = jnp.where(i < n_ctx, i % ctx_per, (i - n_ctx) % lat_per)
        per = jnp.where(i < n_ctx, ctx_per, lat_per)
        return j == 0, j == per - 1

    w_x = w_in[:, :inner].astype(BF16)
    w_o = w_in[:, inner:2 * inner].astype(BF16)
    w_gt = w_in[:, 2 * inner:].T.astype(BF16)
    gb_r = gate_b.reshape(ng, 1)
    tile = 256
    wq_t, wk_t, wv_t = (_blockdiag_tiles(w, tile) for w in (wq, wk, wv))
    wk_t = wk_t.transpose(0, 2, 1)
    r8 = tm // halo
    last8 = rows // halo - 1

    tok = lambda w, dt: jax.ShapeDtypeStruct((rows, w), dt)
    return pl.pallas_call(
        functools.partial(_ml_proj_kernel, heads=heads, first_last=first_last,
                          k_scale=(inner // heads) ** -0.5),
        out_shape=(tok(inner, BF16), jax.ShapeDtypeStruct((inner, rows), BF16), tok(inner, BF16),
                   tok(inner, BF16), tok(inner, BF16), tok(128, F32),
                   jax.ShapeDtypeStruct((ng, rows), F32)),
        grid=(rows // tm,),
        in_specs=[pl.BlockSpec((tm, d), lambda i: (i, 0)),
                  pl.BlockSpec((halo, d), lambda i: (jnp.maximum(i * r8 - 1, 0), 0)),
                  pl.BlockSpec((halo, d), lambda i: (jnp.minimum((i + 1) * r8, last8), 0)),
                  pl.BlockSpec((1,) + mods.shape[1:], lambda i: (lay.mod_row(i, tm), 0, 0)),
                  _const_spec((1, d)), _const_spec(w_x.shape), _const_spec(w_o.shape),
                  _const_spec(w_gt.shape),
                  _const_spec(gb_r.shape), _const_spec(conv_w.shape), _const_spec((1, inner)),
                  _const_spec(wq_t.shape), _const_spec(wk_t.shape), _const_spec(wv_t.shape)],
        out_specs=tuple([pl.BlockSpec((tm, inner), lambda i: (i, 0)),
                         pl.BlockSpec((inner, tm), lambda i: (0, i))]
                        + [pl.BlockSpec((tm, inner), lambda i: (i, 0))] * 3
                        + [pl.BlockSpec((tm, 128), lambda i: (i, 0)),
                           pl.BlockSpec((ng, tm), lambda i: (0, i))]),
        scratch_shapes=[pltpu.VMEM((inner // 512, tm + 2 * halo, 512), F32)],
        compiler_params=_cparams(("arbitrary",)),
        name="mlstm_in_proj",
    )(stream, stream, stream, mods, g.reshape(1, d), w_x, w_o, w_gt, gb_r,
      conv_w, conv_b.reshape(1, inner), wq_t, wk_t, wv_t)


def _ml_scan_kernel(qf_ref, kf_ref, vf_ref, gcf_ref, grf_ref, qb_ref, kb_ref, vb_ref, gcb_ref,
                    grb_ref, hf_ref, hb_ref, c_ref, cb_ref, m_ref, *, heads):
    i = pl.program_id(1)
    dh = qf_ref.shape[1] // heads
    cols = c_ref.shape[3]
    cw = 256

    @pl.when(i == 0)
    def _():
        c_ref[...] = jnp.zeros(c_ref.shape, F32)
        cb_ref[...] = jnp.zeros(cb_ref.shape, BF16)
        m_ref[...] = jnp.zeros(m_ref.shape, F32)

    ii = lax.broadcasted_iota(jnp.int32, (CHUNK, CHUNK), 0)
    jj = lax.broadcasted_iota(jnp.int32, (CHUNK, CHUNK), 1)
    one_col = jnp.where(lax.broadcasted_iota(jnp.int32, (CHUNK, cols - dh), 1) == 0,
                        1.0, 0.0).astype(BF16)
    for d, (q_ref, k_ref, v_ref, gc_ref, gr_ref, h_ref) in enumerate(
            ((qf_ref, kf_ref, vf_ref, gcf_ref, grf_ref, hf_ref),
             (qb_ref, kb_ref, vb_ref, gcb_ref, grb_ref, hb_ref))):
        mask = (jj <= ii) if d == 0 else (jj >= ii)
        end = CHUNK - 1 if d == 0 else 0
        for h in range(heads):
            hs = slice(h * dh, (h + 1) * dh)
            ki, kb = (2 * d) * heads + h, (2 * d + 1) * heads + h
            q, kt, v = q_ref[:, hs], k_ref[hs, :], v_ref[:, hs]
            b_c = gc_ref[:, kb:kb + 1]
            ig_r = gr_ref[ki:ki + 1, :]
            b_r = gr_ref[kb:kb + 1, :]
            m_prev = m_ref[d, h, 0:1, 0:1]
            log_w = jnp.where(mask, b_c - b_r + ig_r, -jnp.inf)
            log_prev = b_c + m_prev
            m_row = jnp.maximum(log_prev, jnp.max(log_w, axis=-1, keepdims=True))
            w = jnp.exp(log_w - m_row)
            w_prev = jnp.exp(log_prev - m_row)
            s = _dot(q, kt) * w
            qc = _dot(q, cb_ref[d, h])
            num = _dot(s.astype(BF16), v) + w_prev * qc[:, :dh]
            den = jnp.sum(s, axis=-1, keepdims=True) + w_prev * qc[:, dh:dh + 1]
            h_ref[:, hs] = (num / jnp.maximum(jnp.abs(den), jnp.exp(-m_row))).astype(h_ref.dtype)
            b_last = b_r[:, end:end + 1]
            log_k = b_last - b_r + ig_r
            m_new = jnp.maximum(b_last + m_prev, jnp.max(log_k, axis=-1, keepdims=True))
            w_k = jnp.exp(log_k - m_new)
            w_c = jnp.exp(b_last + m_prev - m_new)
            kwt = (kt.astype(F32) * w_k).astype(BF16)
            v_ext = jnp.concatenate([v, one_col], axis=1)
            for c0 in range(0, cols, cw):
                c1 = min(c0 + cw, cols)
                cnew = w_c * c_ref[d, h, :, c0:c1] + _dot(kwt, v_ext[:, c0:c1])
                c_ref[d, h, :, c0:c1] = cnew
                cb_ref[d, h, :, c0:c1] = cnew.astype(BF16)
            m_ref[d, h] = jnp.broadcast_to(m_new, m_ref.shape[2:])


def _ml_scan(lay, q, k, v, gc, gr, heads):
    inner = q.shape[1]
    dh = inner // heads
    ncc, nlc = lay.ctx_len // CHUNK, lay.seq // CHUNK
    ng = gr.shape[0]
    ins = []
    for f in (lay.chunk_fwd, lay.chunk_bwd):
        tok = pl.BlockSpec((CHUNK, inner), lambda b, i, f=f: (f(b, i), 0))
        ins += [tok, pl.BlockSpec((inner, CHUNK), lambda b, i, f=f: (0, f(b, i))), tok]
        ins += [pl.BlockSpec((CHUNK, 128), lambda b, i, f=f: (f(b, i), 0)),
                pl.BlockSpec((ng, CHUNK), lambda b, i, f=f: (0, f(b, i)))]
    outs = tuple(pl.BlockSpec((CHUNK, inner), lambda b, i, f=f: (f(b, i), 0))
                 for f in (lay.chunk_fwd, lay.chunk_bwd))
    return pl.pallas_call(
        functools.partial(_ml_scan_kernel, heads=heads),
        out_shape=(jax.ShapeDtypeStruct((lay.rows, inner), BF16),) * 2,
        grid=(lay.batch, ncc + nlc),
        in_specs=ins,
        out_specs=outs,
        scratch_shapes=[pltpu.VMEM((2, heads, dh, dh + 128), F32),
                        pltpu.VMEM((2, heads, dh, dh + 128), BF16),
                        pltpu.VMEM((2, heads, 8, 128), F32)],
        compiler_params=_cparams(("arbitrary", "arbitrary")),
        name="mlstm_scan",
    )(q, k, v, gc, gr, q, k, v, gc, gr)


def _ml_out_kernel(x_ref, m_ref, hf_ref, hb_ref, xc_ref, op_ref, gn_ref, sk_ref, wout_ref,
                   o_ref, *, head_dim):
    m = m_ref[0]
    hsum = hf_ref[...].astype(F32) + hb_ref[...].astype(F32)
    hn = _group_norm(hsum, head_dim) * gn_ref[...] + sk_ref[...] * xc_ref[...].astype(F32)
    y = _dot((_sigmoid(op_ref[...].astype(F32)) * hn).astype(BF16), wout_ref[...])
    o_ref[...] = x_ref[...] + m[2:3] * y


def _ml_out(lay, stream, mods, hf, hb, xc, o_pre, gn_g, skip, w_out, heads):
    rows, d = stream.shape
    inner = hf.shape[1]
    tm = ML_ROW_TILE
    sk = lay.ctx_rows // tm
    n_tiles = rows // tm - sk

    def row(w):
        return pl.BlockSpec((tm, w), lambda i: (i + sk, 0))

    return pl.pallas_call(
        functools.partial(_ml_out_kernel, head_dim=inner // heads),
        out_shape=jax.ShapeDtypeStruct((n_tiles * tm, d), F32),
        grid=(n_tiles,),
        in_specs=[row(d),
                  pl.BlockSpec((1,) + mods.shape[1:], lambda i: (lay.mod_row(i + sk, tm), 0, 0)),
                  row(inner), row(inner), row(inner), row(inner),
                  _const_spec((1, inner)), _const_spec((1, inner)), _const_spec(w_out.shape)],
        out_specs=pl.BlockSpec((tm, d), lambda i: (i, 0)),
        compiler_params=_cparams(("arbitrary",)),
        name="mlstm_out",
    )(stream, mods, hf, hb, xc, o_pre, gn_g.reshape(1, inner), skip.reshape(1, inner), w_out)


def kernel(x, c, ctx, c_ctx, mod_w, mod_b, norm_mix_g, norm_mlp_g, mlp_w1, mlp_w2, final_norm_g,
           ev_w_in, ev_w_out, s5_lambda_re, s5_lambda_im, s5_log_step, s5_b_re, s5_b_im, s5_c_re,
           s5_c_im, s5_d, s5_w_glu, s5_b_glu, ret_decay_logit, ret_gn_g,
           ml_w_in, ml_gate_b, ml_conv_w, ml_conv_b, ml_wq, ml_wk, ml_wv, ml_gn_g, ml_skip,
           ml_w_out):
    bsz, seq, d = x.shape
    ctx_len = ctx.shape[1]
    depth = mod_w.shape[0]
    assert depth == 2, "one even (S5 + retention) layer followed by one mLSTM layer"
    lay = _Layout(bsz, ctx_len, seq)
    assert lay.ctx_rows % ROW_TILE == 0 and seq % ROW_TILE == 0
    assert ctx_len % (S5_SEGS * S5_TB) == 0 and seq % (S5_SEGS * S5_TB) == 0
    assert ctx_len % CHUNK == 0 and seq % CHUNK == 0

    c_all = jnp.concatenate([c, c_ctx[None, :], jnp.zeros((8 - bsz - 1, d), F32)], axis=0)
    mods = _mod_table(c_all, mod_w, mod_b)
    stream = jnp.concatenate([ctx.reshape(bsz * ctx_len, d), x.reshape(bsz * seq, d)], axis=0)

    ret_w = ret_gn_g.shape[1]
    ret_heads = ret_decay_logit.shape[2]
    p0 = _proj(lay, stream, mods[0], norm_mix_g[0], ev_w_in[0].astype(BF16))
    yf, yb = _s5_mixer(lay, p0, s5_lambda_re[0], s5_lambda_im[0], s5_log_step[0],
                       s5_b_re[0], s5_b_im[0], s5_c_re[0], s5_c_im[0])
    rf, rb = _retention(lay, p0, ret_decay_logit[0], ret_w)
    stream = _even_out(lay, stream, mods[0], p0, yf, yb, rf, rb, s5_d[0],
                       s5_w_glu[0].astype(BF16), s5_b_glu[0], ret_gn_g[0],
                       ev_w_out[0].astype(BF16), ret_heads)
    stream = _mlp(lay, stream, mods[0], norm_mlp_g[0], mlp_w1[0].astype(BF16),
                  mlp_w2[0].astype(BF16), final_norm_g, latent_only=False, final=False)

    ml_heads = ml_gate_b.shape[1] // 4
    q, k, v, o_pre, xc, gc, gr = _ml_proj(lay, stream, mods[1], norm_mix_g[1], ml_w_in[0],
                                          ml_gate_b[0], ml_conv_w[0], ml_conv_b[0],
                                          ml_wq[0], ml_wk[0], ml_wv[0], ml_heads)
    hf, hb = _ml_scan(lay, q, k, v, gc, gr, ml_heads)
    lat = _ml_out(lay, stream, mods[1], hf, hb, xc, o_pre, ml_gn_g[0], ml_skip[0],
                  ml_w_out[0].astype(BF16), ml_heads)
    lay_lat = _Layout(bsz, 0, seq)
    out = _mlp(lay_lat, lat, mods[1], norm_mlp_g[1], mlp_w1[1].astype(BF16),
               mlp_w2[1].astype(BF16), final_norm_g, latent_only=False, final=True)
    return out.reshape(bsz, seq, d)
```

```python
import functools
import math

import jax
import jax.numpy as jnp
from jax import lax
from jax.experimental import pallas as pl
from jax.experimental.pallas import tpu as pltpu

F32 = jnp.float32
BF16 = jnp.bfloat16

EPS = 1e-6
GRID_W = 64
ROPE_BASE = 10000.0
CHUNK = 256
RET_CHUNK = 256
ML_QKV_BLOCK = 4
V7X_SCOPED_VMEM_BYTES = 60000 * 1024
S5_SEGS = 4
S5_TB = 32
S5_MATMUL_AHEAD = 1
ROW_TILE = 512
ML_ROW_TILE = 256
CONV_HALO = 8
ML_COL_CHUNK = 512
ML_PROJ_AHEAD = 1


def _cparams(sem, vmem_bytes=V7X_SCOPED_VMEM_BYTES):
    return pltpu.CompilerParams(dimension_semantics=sem, vmem_limit_bytes=vmem_bytes)


def _const_spec(shape):
    nd = len(shape)
    return pl.BlockSpec(shape, lambda *_: (0,) * nd, pipeline_mode=pl.Buffered(1))


def _dot(a, b):
    return jnp.dot(a, b, preferred_element_type=F32)


def _dot_nt(a, b):
    return lax.dot_general(a, b, (((1,), (1,)), ((), ())), preferred_element_type=F32)


def _dot_tn(a, b):
    return lax.dot_general(a, b, (((0,), (0,)), ((), ())), preferred_element_type=F32)


def _dot_f32(a, b):
    return jnp.dot(a, b, preferred_element_type=F32, precision=lax.Precision.HIGHEST)


def _sigmoid(x):
    return jax.nn.sigmoid(x)


def _silu(x):
    return x * jax.nn.sigmoid(x)


def _log_sigmoid(x):
    return jnp.minimum(x, 0.0) - jnp.log1p(jnp.exp(-jnp.abs(x)))


def _rms_mod(x, g, shift, scale):
    y = x * lax.rsqrt(jnp.mean(x * x, axis=-1, keepdims=True) + EPS) * g
    return y * (1.0 + scale) + shift


def _group_norm(x, width):
    outs = []
    for h in range(x.shape[-1] // width):
        xh = x[:, h * width:(h + 1) * width]
        xc = xh - jnp.mean(xh, axis=-1, keepdims=True)
        var = jnp.mean(xc * xc, axis=-1, keepdims=True)
        outs.append(xc * lax.rsqrt(var + EPS))
    return jnp.concatenate(outs, axis=-1)


class _Layout:
    def __init__(self, batch, ctx_len, seq):
        self.batch, self.ctx_len, self.seq = batch, ctx_len, seq
        self.ctx_rows = batch * ctx_len
        self.rows = batch * (ctx_len + seq)

    def mod_row(self, tile, tile_rows):
        n_ctx = self.ctx_rows // tile_rows
        return jnp.where(tile < n_ctx, self.batch, (tile - n_ctx) // (self.seq // tile_rows))

    def chunk_fwd(self, b, i, chunk=CHUNK):
        ncc, nlc = self.ctx_len // chunk, self.seq // chunk
        return jnp.where(i < ncc, b * ncc + i, self.batch * ncc + b * nlc + (i - ncc))

    def chunk_bwd(self, b, i, chunk=CHUNK):
        ncc, nlc = self.ctx_len // chunk, self.seq // chunk
        return jnp.where(i < ncc, b * ncc + (ncc - 1 - i),
                         self.batch * ncc + b * nlc + (nlc - 1 - (i - ncc)))


def _mod_kernel(c_ref, w_ref, b_ref, o_ref):
    s = _silu(c_ref[...]).astype(BF16)
    o_ref[0] = _dot(s, w_ref[0].astype(BF16)) + b_ref[0]


def _mod_table(c_all, mod_w, mod_b):
    depth, d, n = mod_w.shape
    tn = 1536
    out = pl.pallas_call(
        _mod_kernel,
        out_shape=jax.ShapeDtypeStruct((depth, c_all.shape[0], n), F32),
        grid=(depth, n // tn),
        in_specs=[pl.BlockSpec(c_all.shape, lambda l, j: (0, 0)),
                  pl.BlockSpec((1, d, tn), lambda l, j: (l, 0, j)),
                  pl.BlockSpec((1, 1, tn), lambda l, j: (l, 0, j))],
        out_specs=pl.BlockSpec((1, c_all.shape[0], tn), lambda l, j: (l, 0, j)),
        compiler_params=_cparams(("arbitrary", "arbitrary")),
        name="mod_table",
    )(c_all, mod_w, mod_b.reshape(depth, 1, n))
    return out.reshape(depth, c_all.shape[0], n // d, d)


def _two_source(lay, tm, width):
    n_ctx = lay.ctx_rows // tm
    return [pl.BlockSpec((tm, width), lambda i: (jnp.minimum(i, n_ctx - 1), 0)),
            pl.BlockSpec((tm, width), lambda i: (jnp.maximum(i - n_ctx, 0), 0))]


def _proj_kernel(xc_ref, xl_ref, m_ref, g_ref, w_ref, uc_ref, ul_ref, o_ref, *, n_ctx, tn):
    is_ctx = pl.program_id(0) < n_ctx
    m = m_ref[0]
    x = jnp.where(is_ctx, xc_ref[...], xl_ref[...])
    h = _rms_mod(x, g_ref[...], m[0:1], m[1:2]).astype(BF16)
    uw = uc_ref.shape[1]
    u = _dot(h, w_ref[:, 0:uw]).astype(uc_ref.dtype)

    @pl.when(is_ctx)
    def _():
        uc_ref[...] = u

    @pl.when(jnp.logical_not(is_ctx))
    def _():
        ul_ref[...] = u

    for c in range(o_ref.shape[1] // tn):
        o_ref[:, c * tn:(c + 1) * tn] = _dot(h, w_ref[:, uw + c * tn:uw + (c + 1) * tn])


def _proj(lay, ctx2d, x2d, mods, g, w, uw):
    d = x2d.shape[1]
    n = w.shape[1] - uw
    tm = ROW_TILE
    n_ctx = lay.ctx_rows // tm
    return pl.pallas_call(
        functools.partial(_proj_kernel, n_ctx=n_ctx, tn=512),
        out_shape=(jax.ShapeDtypeStruct((lay.ctx_rows, uw), BF16),
                   jax.ShapeDtypeStruct((x2d.shape[0], uw), BF16),
                   jax.ShapeDtypeStruct((lay.rows, n), F32)),
        grid=(lay.rows // tm,),
        in_specs=_two_source(lay, tm, d) + [
            pl.BlockSpec((1,) + mods.shape[1:], lambda i: (lay.mod_row(i, tm), 0, 0)),
            _const_spec((1, d)),
            _const_spec(w.shape)],
        out_specs=tuple(_two_source(lay, tm, uw) + [pl.BlockSpec((tm, n), lambda i: (i, 0))]),
        compiler_params=_cparams(("arbitrary",)),
        name="even_in_proj",
    )(ctx2d, x2d, mods, g.reshape(1, d), w)


def _mlp_kernel(x_ref, m_ref, g_ref, w1_ref, w2_ref, o_ref, *, hc):
    x = x_ref[...]
    m = m_ref[0]
    h = _rms_mod(x, g_ref[...], m[3:4], m[4:5]).astype(BF16)
    acc = jnp.zeros(x.shape, F32)
    for c in range(w1_ref.shape[1] // hc):
        a = _dot(h, w1_ref[:, c * hc:(c + 1) * hc])
        a = jnp.square(jnp.maximum(a, 0.0)).astype(BF16)
        acc = acc + _dot(a, w2_ref[c * hc:(c + 1) * hc, :])
    o_ref[...] = x + m[5:6] * acc


def _layer_spec(w, layer):
    nd = w.ndim - 1
    return pl.BlockSpec((None,) + w.shape[1:], lambda *_: (layer,) + (0,) * nd,
                        pipeline_mode=pl.Buffered(1))


def _mlp(lay, stream, mods, g, w1, w2, layer):
    rows, d = stream.shape
    tm = ROW_TILE
    return pl.pallas_call(
        functools.partial(_mlp_kernel, hc=512),
        out_shape=jax.ShapeDtypeStruct((rows, d), F32),
        grid=(rows // tm,),
        in_specs=[pl.BlockSpec((tm, d), lambda i: (i, 0)),
                  pl.BlockSpec((1,) + mods.shape[1:], lambda i: (lay.mod_row(i, tm), 0, 0)),
                  _const_spec((1, d)),
                  _layer_spec(w1, layer),
                  _layer_spec(w2, layer)],
        out_specs=pl.BlockSpec((tm, d), lambda i: (i, 0)),
        compiler_params=_cparams(("arbitrary",)),
        name="mlp",
    )(stream, mods, g.reshape(1, d), w1, w2)


def _s5_disc_kernel(lre_ref, lim_ref, ls_ref, bre_ref, bim_ref,
                    lam_ref, bbar_ref, pow_ref, *, seg_ctx, seg_lat):
    lre, lim = lre_ref[...], lim_ref[...]
    step = jnp.exp(ls_ref[...])
    mag = jnp.exp(lre * step)
    ang = lim * step
    lbr, lbi = mag * jnp.cos(ang), mag * jnp.sin(ang)
    lam_ref[0], lam_ref[1] = lbr, lbi
    den = lre * lre + lim * lim
    nr, ni = lbr - 1.0, lbi
    cr = (nr * lre + ni * lim) / den
    ci = (ni * lre - nr * lim) / den
    bre, bim = bre_ref[...], bim_ref[...]
    bbar_ref[0] = cr[:, None] * bre - ci[:, None] * bim
    bbar_ref[1] = cr[:, None] * bim + ci[:, None] * bre
    for k, n in enumerate((seg_ctx, seg_lat)):
        mag_n = jnp.exp(lre * step * n)
        ang_n = lim * step * n
        pow_ref[2 * k] = mag_n * jnp.cos(ang_n)
        pow_ref[2 * k + 1] = mag_n * jnp.sin(ang_n)


def _s5_discretize(lam_re, lam_im, log_step, b_re, b_im, seg_ctx, seg_lat):
    _, g, p = lam_re.shape
    n = b_re.shape[-1]
    gp = g * p
    lt = gp // 128
    shp = (2, lt, 128)
    ls = jnp.broadcast_to(log_step[:, :, None], (2, g, p)).reshape(shp)
    bre = b_re.reshape(2, gp, n).transpose(0, 2, 1).reshape(2, n, lt, 128)
    bim = b_im.reshape(2, gp, n).transpose(0, 2, 1).reshape(2, n, lt, 128)
    lam, bbar, pw = pl.pallas_call(
        functools.partial(_s5_disc_kernel, seg_ctx=seg_ctx, seg_lat=seg_lat),
        out_shape=(jax.ShapeDtypeStruct((2,) + shp, F32),
                   jax.ShapeDtypeStruct((2, 2, n, lt, 128), F32),
                   jax.ShapeDtypeStruct((4,) + shp, F32)),
        name="s5_discretize",
    )(lam_re.reshape(shp), lam_im.reshape(shp), ls, bre, bim)
    return lam.reshape(2, 2, gp), bbar.reshape(2, 2, n, gp), pw.reshape(4, 2, gp)


def _s5_scan_block(u_refs, wb_ref, wc_ref, lam_ref, st_ref, h_ref, y_refs, *, tb, final_pass):
    nkt, nrow, gw = st_ref.shape[1], st_ref.shape[3], st_ref.shape[4]
    lane_tiles = [slice(c * 128, (c + 1) * 128) for c in range(gw // 128)]

    pieces = [(2 * s + d, d, s) for s in range(nkt) for d in range(2)]

    def input_matmul(d, kt):
        u = u_refs[d, :, kt * 128:(kt + 1) * 128]
        return _dot(u, wb_ref[d, kt, 0]), _dot(u, wb_ref[d, kt, 1])

    def scan(hb, d, kt, bu_re, bu_im):
        lr = [jnp.broadcast_to(lam_ref[d, kt, 0, :, cs], (nrow, 128)) for cs in lane_tiles]
        li = [jnp.broadcast_to(lam_ref[d, kt, 1, :, cs], (nrow, 128)) for cs in lane_tiles]
        hr = [st_ref[d, kt, 0, :, cs] for cs in lane_tiles]
        hi = [st_ref[d, kt, 1, :, cs] for cs in lane_tiles]
        for t in range(tb):
            rows = slice(t * nrow, (t + 1) * nrow)
            br, bi = bu_re[rows, :], bu_im[rows, :]
            for c, cs in enumerate(lane_tiles):
                xr = lr[c] * hr[c] - li[c] * hi[c] + br[:, cs]
                xi = lr[c] * hi[c] + li[c] * hr[c] + bi[:, cs]
                if final_pass:
                    h_ref[hb, 0, rows, cs] = xr
                    h_ref[hb, 1, rows, cs] = xi
                hr[c], hi[c] = xr, xi
        for c, cs in enumerate(lane_tiles):
            st_ref[d, kt, 0, :, cs] = hr[c]
            st_ref[d, kt, 1, :, cs] = hi[c]

    ahead = [input_matmul(d, kt) for _, d, kt in pieces[:S5_MATMUL_AHEAD]]
    for i, (hb, d, kt) in enumerate(pieces):
        if i + S5_MATMUL_AHEAD < len(pieces):
            _, d2, kt2 = pieces[i + S5_MATMUL_AHEAD]
            ahead.append(input_matmul(d2, kt2))
        scan(hb, d, kt, *ahead.pop(0))
        if final_pass:
            y_refs[d, :, kt * 128:(kt + 1) * 128] = (
                _dot(h_ref[hb, 0].astype(BF16), wc_ref[d, kt, 0])
                - _dot(h_ref[hb, 1].astype(BF16), wc_ref[d, kt, 1]))


def _s5_scan_kernel(*refs, tb, ncb, nblk, final_pass):
    if final_pass:
        (ucf_ref, ulf_ref, ucb_ref, ulb_ref, perm_ref, permt_ref, wb_ref, wc_ref, lam_ref,
         pow_ref, fin_ref, ycf_ref, ylf_ref, ycb_ref, ylb_ref,
         st_ref, u_scr, init_ref, h_ref, y_scr) = refs
    else:
        (ucf_ref, ulf_ref, ucb_ref, ulb_ref, perm_ref, wb_ref, lam_ref, fin_ref,
         st_ref, u_scr) = refs
        wc_ref, h_ref, y_scr = None, None, None
    i = pl.program_id(0)
    part = jnp.where(i < ncb, 0, 1)
    first = jnp.logical_or(i == 0, i == ncb)
    last = jnp.logical_or(i == ncb - 1, i == nblk - 1)
    nseg = S5_SEGS
    nkt, nrow = st_ref.shape[1], st_ref.shape[3]

    @pl.when(i == 0)
    def _():
        if final_pass:
            for d in range(2):
                for kt in range(nkt):
                    for b in range(nrow // nseg):
                        cr = jnp.zeros((1, st_ref.shape[4]), F32)
                        ci = jnp.zeros((1, st_ref.shape[4]), F32)
                        for prt in range(2):
                            pr = pow_ref[d, prt, kt, 0]
                            pi = pow_ref[d, prt, kt, 1]
                            for s in range(nseg):
                                seg = s if d == 0 else nseg - 1 - s
                                r = b * nseg + seg
                                init_ref[d, prt, kt, 0, r:r + 1, :] = cr
                                init_ref[d, prt, kt, 1, r:r + 1, :] = ci
                                fr = fin_ref[d, prt, kt, 0, r:r + 1, :]
                                fi = fin_ref[d, prt, kt, 1, r:r + 1, :]
                                cr, ci = pr * cr - pi * ci + fr, pr * ci + pi * cr + fi

    @pl.when(first)
    def _():
        if final_pass:
            for d in range(2):
                st_ref[d] = jnp.where(part == 0, init_ref[d, 0], init_ref[d, 1])
        else:
            st_ref[...] = jnp.zeros(st_ref.shape, F32)

    blk = tb * nrow
    for d, (uc_ref, ul_ref) in enumerate(((ucf_ref, ulf_ref), (ucb_ref, ulb_ref))):
        u_nat = jnp.where(part == 0, uc_ref[...], ul_ref[...]).reshape(blk, uc_ref.shape[2])
        u_scr[d] = _dot(perm_ref[d], u_nat).astype(BF16)

    _s5_scan_block(u_scr, wb_ref, wc_ref, lam_ref, st_ref, h_ref, y_scr,
                   tb=tb, final_pass=final_pass)

    if final_pass:
        for d, (yc_ref, yl_ref) in enumerate(((ycf_ref, ylf_ref), (ycb_ref, ylb_ref))):
            y_nat = _dot(permt_ref[d], y_scr[d].astype(BF16)).astype(BF16).reshape(yc_ref.shape)

            @pl.when(part == 0)
            def _(yc_ref=yc_ref, y_nat=y_nat):
                yc_ref[...] = y_nat

            @pl.when(part == 1)
            def _(yl_ref=yl_ref, y_nat=y_nat):
                yl_ref[...] = y_nat

    if not final_pass:
        @pl.when(last)
        def _():
            for d in range(2):
                for prt in range(2):
                    @pl.when(part == prt)
                    def _(d=d, prt=prt):
                        fin_ref[d, prt] = st_ref[d]


def _s5_scan(u_c, u_l, perm, wb, wc, lam, pw, fin, *, final_pass):
    nkt, nrow, gw = fin.shape[2], fin.shape[4], fin.shape[5]
    uw = u_c.shape[2]
    tb = S5_TB
    ncb, nlb = u_c.shape[1] // tb, u_l.shape[1] // tb
    nblk = ncb + nlb

    def c_f(i):
        return jnp.minimum(i, ncb - 1)

    def l_f(i):
        return jnp.clip(i - ncb, 0, nlb - 1)

    def spec(pos):
        return pl.BlockSpec((nrow, tb, uw), lambda i: (0, pos(i), 0))

    tiles = [spec(c_f), spec(l_f), spec(lambda i: ncb - 1 - c_f(i)), spec(lambda i: nlb - 1 - l_f(i))]
    kern = functools.partial(_s5_scan_kernel, tb=tb, ncb=ncb, nblk=nblk, final_pass=final_pass)
    state = pltpu.VMEM((2, nkt, 2, nrow, gw), F32)
    u_scr = pltpu.VMEM((2, tb * nrow, uw), BF16)
    if final_pass:
        perm_t = perm.transpose(0, 2, 1)
        return pl.pallas_call(
            kern,
            out_shape=(jax.ShapeDtypeStruct(u_c.shape, BF16), jax.ShapeDtypeStruct(u_l.shape, BF16)) * 2,
            grid=(nblk,),
            in_specs=tiles + [_const_spec(perm.shape), _const_spec(perm.shape),
                              _const_spec(wb.shape), _const_spec(wc.shape),
                              _const_spec(lam.shape), _const_spec(pw.shape), _const_spec(fin.shape)],
            out_specs=tuple(tiles),
            scratch_shapes=[state, u_scr, pltpu.VMEM(fin.shape, F32),
                            pltpu.VMEM((2 * nkt, 2, tb * nrow, gw), F32),
                            pltpu.VMEM((2, tb * nrow, uw), F32)],
            compiler_params=_cparams(("arbitrary",)),
            name="s5_scan_out",
        )(u_c, u_l, u_c, u_l, perm, perm_t, wb, wc, lam, pw, fin)
    return pl.pallas_call(
        kern,
        out_shape=jax.ShapeDtypeStruct(fin.shape, F32),
        grid=(nblk,),
        in_specs=tiles + [_const_spec(perm.shape), _const_spec(wb.shape), _const_spec(lam.shape)],
        out_specs=pl.BlockSpec(fin.shape, lambda i: (0,) * len(fin.shape)),
        scratch_shapes=[state, u_scr],
        compiler_params=_cparams(("arbitrary",)),
        name="s5_scan_state",
    )(u_c, u_l, u_c, u_l, perm, wb, lam)


def _s5_mixer(lay, u_ctx, u_lat, lam_re, lam_im, log_step, b_re, b_im, c_re, c_im):
    bsz = lay.batch
    _, g, pst = lam_re.shape
    ngrp = b_re.shape[-1]
    uw = g * ngrp
    half = g * pst
    nseg = S5_SEGS
    seg_ctx, seg_lat = lay.ctx_len // nseg, lay.seq // nseg
    nrow = bsz * nseg

    lam, bbar, pw = _s5_discretize(lam_re, lam_im, log_step, b_re, b_im, seg_ctx, seg_lat)

    kt_g = 128 // ngrp
    kt_n = g // kt_g
    bb = bbar.reshape(2, 2, ngrp, kt_n, kt_g, pst)
    eye = jnp.eye(kt_g, dtype=bool)
    wb = jnp.where(eye[None, None, None, :, None, :, None],
                   bb.transpose(0, 1, 3, 2, 4, 5)[:, :, :, None, :, :, :], 0.0)
    wb = wb.reshape(2, 2, kt_n, kt_g * ngrp, kt_g * pst)
    wb = jnp.stack([wb[0], wb[1]], axis=2).astype(BF16)

    gw = kt_g * pst

    def c_tiles(c):
        cc = c.reshape(2, kt_n, kt_g, ngrp, pst).transpose(0, 1, 2, 4, 3)
        w = jnp.where(eye[None, None, :, None, :, None], cc[:, :, :, :, None, :], 0.0)
        return w.reshape(2, kt_n, gw, kt_g * ngrp)

    wc = jnp.stack([c_tiles(c_re), c_tiles(c_im)], axis=2).astype(BF16)

    def grouped(a, lead):
        a = a.reshape(lead + (2, 2, kt_n, 1, gw))
        perm = (len(lead) + 1,) + tuple(range(len(lead))) + (len(lead) + 2, len(lead), len(lead) + 3,
                                                             len(lead) + 4)
        return a.transpose(perm)

    lam_g = grouped(lam, ())
    pw_g = grouped(pw.reshape(2, 2, 2, half), (2,))

    tb = S5_TB
    out_row = jnp.arange(tb * nrow)[:, None]
    in_row = jnp.arange(tb * nrow)[None, :]
    t_o, r_o = out_row // nrow, out_row % nrow
    perm = jnp.stack([(in_row == r_o * tb + tt).astype(BF16) for tt in (t_o, tb - 1 - t_o)])

    u_c = u_ctx.reshape(nrow, seg_ctx, uw)
    u_l = u_lat.reshape(nrow, seg_lat, uw)
    fin0 = jnp.zeros((2, 2, kt_n, 2, nrow, gw), F32)
    fin = _s5_scan(u_c, u_l, perm, wb, None, lam_g, None, fin0, final_pass=False)
    ycf, ylf, ycb, ylb = _s5_scan(u_c, u_l, perm, wb, wc, lam_g, pw_g, fin, final_pass=True)
    flat = lambda y: y.reshape(-1, uw)
    return flat(ycf), flat(ylf), flat(ycb), flat(ylb)


def _rope_tables(chunk_idx, col_cos, col_sin, is_lat):
    chunk = col_cos.shape[0]
    lane = lax.broadcasted_iota(jnp.int32, (8, 128), 1)
    sub = lax.broadcasted_iota(jnp.int32, (8, 128), 0)
    quarter = 32
    inv = jnp.exp((lane & (quarter - 1)).astype(F32) * (-math.log(ROPE_BASE) / quarter))
    rows_per_chunk = chunk // GRID_W
    ang = (chunk_idx * rows_per_chunk + sub).astype(F32) * inv
    rc, rs = jnp.cos(ang), jnp.sin(ang)
    tok = lax.broadcasted_iota(jnp.int32, (chunk, 128), 0)
    lane_t = lax.broadcasted_iota(jnp.int32, (chunk, 128), 1)
    row_cos = jnp.zeros((chunk, 128), F32)
    row_sin = jnp.zeros((chunk, 128), F32)
    for r in range(rows_per_chunk):
        sel = (tok // GRID_W) == r
        row_cos = jnp.where(sel, rc[r:r + 1, :], row_cos)
        row_sin = jnp.where(sel, rs[r:r + 1, :], row_sin)
    is_col = (lane_t & quarter) != 0
    cos_t = jnp.where(is_col, col_cos, row_cos)
    sin_t = jnp.where(is_col, col_sin, row_sin)
    sin_t = jnp.where(lane_t < 64, -sin_t, sin_t)
    cos_t = jnp.where(is_lat, cos_t, 1.0)
    sin_t = jnp.where(is_lat, sin_t, 0.0)
    return cos_t, sin_t


def _ret_kernel(dl_ref, qf_ref, kf_ref, vf_ref, qb_ref, kb_ref, vb_ref, of_ref, ob_ref,
                st_ref, intra_ref, tab_ref, col_ref, *, heads, ncc, nlc):
    i = pl.program_id(1)
    dh = qf_ref.shape[1] // heads
    chunk = qf_ref.shape[0]

    @pl.when(i == 0)
    def _():
        st_ref[...] = jnp.zeros(st_ref.shape, F32)
        ii = lax.broadcasted_iota(jnp.int32, (chunk, chunk), 0)
        jj = lax.broadcasted_iota(jnp.int32, (chunk, chunk), 1)
        relf = (ii - jj).astype(F32)
        rowf = lax.broadcasted_iota(jnp.int32, (chunk, dh), 0).astype(F32)
        for d in range(2):
            for h in range(heads):
                lg = _log_sigmoid(jnp.full((chunk, chunk), dl_ref[d, h], F32))
                lgv = _log_sigmoid(jnp.full((chunk, dh), dl_ref[d, h], F32))
                if d == 0:
                    mask = ii >= jj
                    intra = jnp.where(mask, jnp.exp(jnp.where(mask, relf, 0.0) * lg), 0.0)
                    dq = jnp.exp((rowf + 1.0) * lgv)
                    dk = jnp.exp((chunk - 1.0 - rowf) * lgv)
                else:
                    mask = jj > ii
                    intra = jnp.where(mask, jnp.exp(jnp.where(mask, -relf, 0.0) * lg), 0.0)
                    dq = jnp.exp((chunk - rowf) * lgv)
                    dk = jnp.exp(rowf * lgv)
                intra_ref[d, h] = intra
                tab_ref[d, h, 0] = dq
                tab_ref[d, h, 1] = dk
                tab_ref[d, h, 2] = jnp.exp(chunk * lgv)
        lane = lax.broadcasted_iota(jnp.int32, (chunk, 128), 1)
        tok = lax.broadcasted_iota(jnp.int32, (chunk, 128), 0)
        inv = jnp.exp((lane & 31).astype(F32) * (-math.log(ROPE_BASE) / 32))
        ang = (tok % GRID_W).astype(F32) * inv
        col_ref[0] = jnp.cos(ang)
        col_ref[1] = jnp.sin(ang)

    is_lat = i >= ncc
    scale = dh ** -0.5
    for d, (q_ref, k_ref, v_ref, o_ref) in enumerate(((qf_ref, kf_ref, vf_ref, of_ref),
                                                      (qb_ref, kb_ref, vb_ref, ob_ref))):
        cidx = (i - ncc) if d == 0 else nlc - 1 - (i - ncc)
        cos_t, sin_t = _rope_tables(jnp.maximum(cidx, 0), col_ref[0], col_ref[1], is_lat)
        for h in range(heads):
            q = q_ref[:, h * dh:(h + 1) * dh]
            k = k_ref[:, h * dh:(h + 1) * dh]
            v = v_ref[:, h * dh:(h + 1) * dh].astype(BF16)
            q = q * cos_t + pltpu.roll(q, dh // 2, 1) * sin_t
            k = (k * cos_t + pltpu.roll(k, dh // 2, 1) * sin_t) * scale
            qb = q.astype(BF16)
            state = st_ref[d, h]
            s = _dot_nt(qb, k.astype(BF16)) * intra_ref[d, h]
            o = _dot(s.astype(BF16), v) + _dot(qb, state.astype(BF16)) * tab_ref[d, h, 0]
            o_ref[:, h * dh:(h + 1) * dh] = o.astype(o_ref.dtype)
            kdt = (k * tab_ref[d, h, 1]).T.astype(BF16)
            st_ref[d, h] = state * tab_ref[d, h, 2, 0:dh, :] + _dot(kdt, v)


def _retention(lay, p0, decay_logit, width):
    heads = decay_logit.shape[1]
    dh = width // heads
    chunk = RET_CHUNK
    assert dh == 128 and chunk % GRID_W == 0 and chunk >= dh
    assert lay.ctx_len % chunk == 0 and lay.seq % chunk == 0
    ncc, nlc = lay.ctx_len // chunk, lay.seq // chunk
    ucols = (p0.shape[1] - 4 * width) // width
    orders = [functools.partial(f, chunk=chunk) for f in (lay.chunk_fwd, lay.chunk_bwd)]

    def spec(col, idx):
        return pl.BlockSpec((chunk, width), lambda b, i: (idx(b, i), col))

    ins = [spec(ucols + c, f) for f in orders for c in range(3)]
    outs = tuple(pl.BlockSpec((chunk, width), lambda b, i, f=f: (f(b, i), 0)) for f in orders)
    return pl.pallas_call(
        functools.partial(_ret_kernel, heads=heads, ncc=ncc, nlc=nlc),
        out_shape=(jax.ShapeDtypeStruct((lay.rows, width), BF16),) * 2,
        grid=(lay.batch, ncc + nlc),
        in_specs=[pl.BlockSpec(memory_space=pltpu.SMEM)] + ins,
        out_specs=outs,
        scratch_shapes=[pltpu.VMEM((2, heads, dh, dh), F32),
                        pltpu.VMEM((2, heads, chunk, chunk), F32),
                        pltpu.VMEM((2, heads, 3, chunk, dh), F32),
                        pltpu.VMEM((2, chunk, 128), F32)],
        compiler_params=_cparams(("arbitrary", "arbitrary")),
        name="retention",
    )(decay_logit, p0, p0, p0, p0, p0, p0)


def _even_out_kernel(xc_ref, xl_ref, uc_ref, ul_ref, yfc_ref, yfl_ref, ybc_ref, ybl_ref, m_ref,
                     rf_ref, rb_ref, g_ref, d_ref, wglu_ref, bglu_ref, gn_ref, wout_ref, o_ref,
                     *, n_ctx, head_dim):
    is_ctx = pl.program_id(0) < n_ctx

    def pick(c_ref, l_ref):
        return jnp.where(is_ctx, c_ref[...], l_ref[...])

    m = m_ref[0]
    uw = uc_ref.shape[1]
    x = pick(xc_ref, xl_ref)
    s5 = (pick(uc_ref, ul_ref).astype(F32) * d_ref[...]
          + pick(yfc_ref, yfl_ref).astype(F32) + pick(ybc_ref, ybl_ref).astype(F32))
    ab = _dot(jax.nn.gelu(s5).astype(BF16), wglu_ref[...]) + bglu_ref[...]
    s5_o = ab[:, :uw] * _sigmoid(ab[:, uw:])
    ret = _group_norm(rf_ref[...].astype(F32) + rb_ref[...].astype(F32), head_dim) * gn_ref[...]
    ret_o = ret * _silu(g_ref[...])
    y = _dot(s5_o.astype(BF16), wout_ref[0:uw, :]) + _dot(ret_o.astype(BF16), wout_ref[uw:, :])
    o_ref[...] = x + m[2:3] * y


def _even_out(lay, ctx2d, x2d, mods, u_parts, y_parts, p0, rf, rb, s5_d, w_glu, b_glu, gn_g,
              w_out, heads):
    d = x2d.shape[1]
    uw = u_parts[0].shape[1]
    rw = rf.shape[1]
    tm = ROW_TILE
    gcol = (p0.shape[1] - rw) // rw

    def row(w, col=0):
        return pl.BlockSpec((tm, w), lambda i: (i, col))

    return pl.pallas_call(
        functools.partial(_even_out_kernel, n_ctx=lay.ctx_rows // tm, head_dim=rw // heads),
        out_shape=jax.ShapeDtypeStruct((lay.rows, d), F32),
        grid=(lay.rows // tm,),
        in_specs=(_two_source(lay, tm, d) + _two_source(lay, tm, uw) * 3
                  + [pl.BlockSpec((1,) + mods.shape[1:], lambda i: (lay.mod_row(i, tm), 0, 0)),
                     row(rw), row(rw), row(rw, gcol),
                     _const_spec((1, uw)), _const_spec(w_glu.shape), _const_spec((1, 2 * uw)),
                     _const_spec((1, rw)), _const_spec(w_out.shape)]),
        out_specs=row(d),
        compiler_params=_cparams(("arbitrary",)),
        name="even_out",
    )(ctx2d, x2d, *u_parts, *y_parts, mods, rf, rb, p0, s5_d.reshape(1, uw), w_glu,
      b_glu.reshape(1, 2 * uw), gn_g.reshape(1, rw), w_out)


def _ml_proj_kernel(x_ref, xp_ref, xn_ref, m_ref, g_ref, wx_ref, wo_ref, wgt_ref,
                    gbt_ref, cw_ref, cb_ref, wq_ref, wk_ref, wv_ref,
                    q_ref, k_ref, v_ref, op_ref, xc_ref, gc_ref, gr_ref, xm_scr,
                    *, heads, first_last, k_scale):
    i = pl.program_id(0)
    tm = x_ref.shape[0]
    halo = xp_ref.shape[0]
    m = m_ref[0]
    first, last = first_last(i)
    g = g_ref[...]
    hf = _rms_mod(x_ref[...], g, m[0:1], m[1:2])
    hp = _rms_mod(xp_ref[...], g, m[0:1], m[1:2]) * jnp.where(first, 0.0, 1.0)
    hn = _rms_mod(xn_ref[...], g, m[0:1], m[1:2]) * jnp.where(last, 0.0, 1.0)
    h = hf.astype(BF16)
    h_ext = jnp.concatenate([hp, hf, hn], axis=0).astype(BF16)

    gates = _dot_nt(wgt_ref[...], h) + gbt_ref[...]
    si = lax.broadcasted_iota(jnp.int32, (tm, tm), 0)
    ti = lax.broadcasted_iota(jnp.int32, (tm, tm), 1)
    same = (si // CHUNK) == (ti // CHUNK)
    upto = jnp.where(jnp.logical_and(same, si <= ti), 1.0, 0.0)
    from_ = jnp.where(jnp.logical_and(same, si >= ti), 1.0, 0.0)
    kind = lax.broadcasted_iota(jnp.int32, gates.shape, 0) // heads
    ls = _log_sigmoid(gates)
    cum_f = _dot_f32(ls, upto)
    cum_b = _dot_f32(ls, from_)
    gr = jnp.where(kind == 1, cum_f, jnp.where(kind == 3, cum_b, gates))
    gr_ref[...] = gr
    pad = jnp.zeros((gc_ref.shape[1] - gr.shape[0], tm), F32)
    gc_ref[...] = jnp.concatenate([gr, pad], axis=0).T

    width = wx_ref.shape[1]
    cn = xm_scr.shape[2]
    taps = cw_ref.shape[0]

    def project(c):
        cs = slice(c * cn, (c + 1) * cn)
        op_ref[:, cs] = _dot(h, wo_ref[:, cs]).astype(op_ref.dtype)
        xm_scr[c] = _dot(h_ext, wx_ref[:, cs])

    for c in range(ML_PROJ_AHEAD):
        project(c)
    for c in range(width // cn):
        cs = slice(c * cn, (c + 1) * cn)
        if c + ML_PROJ_AHEAD < width // cn:
            project(c + ML_PROJ_AHEAD)
        xm = xm_scr[c, halo:halo + tm, :]
        acc = jnp.zeros((tm, cn), F32) + cb_ref[:, cs]
        for t in range(taps):
            off = halo + t - taps // 2
            acc = acc + xm_scr[c, off:off + tm, :] * cw_ref[t:t + 1, cs]
        xc = _silu(acc)
        xc_ref[:, cs] = xc.astype(xc_ref.dtype)
        xcb, xmb = xc.astype(BF16), xm.astype(BF16)
        bw = wq_ref.shape[1]
        for j in range(cn // bw):
            blk = c * (cn // bw) + j
            ls_ = slice(j * bw, (j + 1) * bw)
            os_ = slice(c * cn + j * bw, c * cn + (j + 1) * bw)
            q_ref[:, os_] = _dot(xcb[:, ls_], wq_ref[blk]).astype(BF16)
            k_ref[os_, :] = (_dot_nt(wk_ref[blk], xcb[:, ls_]) * k_scale).astype(BF16)
            v_ref[:, os_] = _dot(xmb[:, ls_], wv_ref[blk]).astype(BF16)


def _blockdiag_tiles(w, tile):
    nb, blk, _ = w.shape
    per = tile // blk
    rows = jnp.tile(w.reshape(nb // per, tile, blk), (1, 1, per))
    a_of_row = jnp.arange(tile)[:, None] // blk
    b_of_lane = jnp.arange(tile)[None, :] // blk
    return jnp.where(a_of_row == b_of_lane, rows, 0.0).astype(BF16)


def _ml_proj(lay, stream, mods, g, w_in, gate_b, conv_w, conv_b, wq, wk, wv, heads):
    rows, d = stream.shape
    inner = conv_w.shape[1]
    ng = 4 * heads
    tm = ML_ROW_TILE
    halo = CONV_HALO
    assert lay.ctx_len % tm == 0 and lay.seq % tm == 0 and tm % CHUNK == 0
    n_ctx = lay.ctx_rows // tm
    ctx_per, lat_per = lay.ctx_len // tm, lay.seq // tm

    def first_last(i):
        j = jnp.where(i < n_ctx, i % ctx_per, (i - n_ctx) % lat_per)
        per = jnp.where(i < n_ctx, ctx_per, lat_per)
        return j == 0, j == per - 1

    w_all = w_in.astype(BF16)
    w_gt = w_in[:, 2 * inner:].T.astype(BF16)
    gb_r = gate_b.reshape(ng, 1)
    tile = 256
    wq_t, wk_t, wv_t = (_blockdiag_tiles(w, tile) for w in (wq, wk, wv))
    wk_t = wk_t.transpose(0, 2, 1)
    r8 = tm // halo
    last8 = rows // halo - 1

    tok = lambda w, dt: jax.ShapeDtypeStruct((rows, w), dt)
    return pl.pallas_call(
        functools.partial(_ml_proj_kernel, heads=heads, first_last=first_last,
                          k_scale=(inner // heads) ** -0.5),
        out_shape=(tok(inner, BF16), jax.ShapeDtypeStruct((inner, rows), BF16), tok(inner, BF16),
                   tok(inner, BF16), tok(inner, BF16), tok(128, F32),
                   jax.ShapeDtypeStruct((ng, rows), F32)),
        grid=(rows // tm,),
        in_specs=[pl.BlockSpec((tm, d), lambda i: (i, 0)),
                  pl.BlockSpec((halo, d), lambda i: (jnp.maximum(i * r8 - 1, 0), 0)),
                  pl.BlockSpec((halo, d), lambda i: (jnp.minimum((i + 1) * r8, last8), 0)),
                  pl.BlockSpec((1,) + mods.shape[1:], lambda i: (lay.mod_row(i, tm), 0, 0)),
                  _const_spec((1, d)),
                  pl.BlockSpec((d, inner), lambda i: (0, 0), pipeline_mode=pl.Buffered(1)),
                  pl.BlockSpec((d, inner), lambda i: (0, 1), pipeline_mode=pl.Buffered(1)),
                  _const_spec(w_gt.shape),
                  _const_spec(gb_r.shape), _const_spec(conv_w.shape), _const_spec((1, inner)),
                  _const_spec(wq_t.shape), _const_spec(wk_t.shape), _const_spec(wv_t.shape)],
        out_specs=tuple([pl.BlockSpec((tm, inner), lambda i: (i, 0)),
                         pl.BlockSpec((inner, tm), lambda i: (0, i))]
                        + [pl.BlockSpec((tm, inner), lambda i: (i, 0))] * 3
                        + [pl.BlockSpec((tm, 128), lambda i: (i, 0)),
                           pl.BlockSpec((ng, tm), lambda i: (0, i))]),
        scratch_shapes=[pltpu.VMEM((inner // ML_COL_CHUNK, tm + 2 * halo, ML_COL_CHUNK), F32)],
        compiler_params=_cparams(("arbitrary",)),
        name="mlstm_in_proj",
    )(stream, stream, stream, mods, g.reshape(1, d), w_all, w_all, w_gt, gb_r,
      conv_w, conv_b.reshape(1, inner), wq_t, wk_t, wv_t)


def _ml_scan_kernel(qf_ref, kf_ref, vf_ref, gcf_ref, grf_ref, qb_ref, kb_ref, vb_ref, gcb_ref,
                    grb_ref, hf_ref, hb_ref, c_ref, cb_ref, m_ref, *, heads):
    i = pl.program_id(1)
    dh = qf_ref.shape[1] // heads
    cols = c_ref.shape[3]
    cw = 256

    @pl.when(i == 0)
    def _():
        c_ref[...] = jnp.zeros(c_ref.shape, F32)
        cb_ref[...] = jnp.zeros(cb_ref.shape, BF16)
        m_ref[...] = jnp.zeros(m_ref.shape, F32)

    ii = lax.broadcasted_iota(jnp.int32, (CHUNK, CHUNK), 0)
    jj = lax.broadcasted_iota(jnp.int32, (CHUNK, CHUNK), 1)
    lane0 = lax.broadcasted_iota(jnp.int32, (CHUNK, cols - dh), 1) == 0
    one_col = jnp.where(lane0, 1.0, 0.0).astype(BF16)
    tok_r = lax.broadcasted_iota(jnp.int32, (1, CHUNK), 1)
    for d, (q_ref, k_ref, v_ref, gc_ref, gr_ref, h_ref) in enumerate(
            ((qf_ref, kf_ref, vf_ref, gcf_ref, grf_ref, hf_ref),
             (qb_ref, kb_ref, vb_ref, gcb_ref, grb_ref, hb_ref))):
        mask = (jj <= ii) if d == 0 else (jj >= ii)
        end = CHUNK - 1 if d == 0 else 0
        for h in range(heads):
            hs = slice(h * dh, (h + 1) * dh)
            ki, kb = (2 * d) * heads + h, (2 * d + 1) * heads + h
            q, kt, v = q_ref[:, hs], k_ref[hs, :], v_ref[:, hs]
            b_c = gc_ref[:, kb:kb + 1]
            ig_r = gr_ref[ki:ki + 1, :]
            b_r = gr_ref[kb:kb + 1, :]
            m_prev = m_ref[d, h, 0:1, 0:1]
            log_w = jnp.where(mask, b_c - b_r + ig_r, -jnp.inf)
            log_prev = b_c + m_prev
            m_row = jnp.maximum(log_prev, jnp.max(log_w, axis=-1, keepdims=True))
            w = jnp.exp(log_w - m_row)
            w_prev = jnp.exp(log_prev - m_row)
            s = _dot(q, kt) * w
            qn = _dot(q, cb_ref[d, h, :, dh:cols])
            qn = jnp.sum(jnp.where(lane0, qn, 0.0), axis=-1, keepdims=True)
            num = _dot(s.astype(BF16), v) + w_prev * _dot(q, cb_ref[d, h, :, 0:dh])
            den = jnp.sum(s, axis=-1, keepdims=True) + w_prev * qn
            h_ref[:, hs] = (num / jnp.maximum(jnp.abs(den), jnp.exp(-m_row))).astype(h_ref.dtype)
            b_last = jnp.sum(jnp.where(tok_r == end, b_r, 0.0), axis=-1, keepdims=True)
            log_k = b_last - b_r + ig_r
            m_new = jnp.maximum(b_last + m_prev, jnp.max(log_k, axis=-1, keepdims=True))
            w_k = jnp.exp(log_k - m_new)
            w_c = jnp.exp(b_last + m_prev - m_new)
            kwt = (kt.astype(F32) * w_k).astype(BF16)
            for c0 in range(0, cols, cw):
                c1 = min(c0 + cw, cols)
                rhs = v_ref[:, h * dh + c0:h * dh + c1] if c1 <= dh else one_col
                cnew = w_c * c_ref[d, h, :, c0:c1] + _dot(kwt, rhs)
                c_ref[d, h, :, c0:c1] = cnew
                cb_ref[d, h, :, c0:c1] = cnew.astype(BF16)
            m_ref[d, h] = jnp.broadcast_to(m_new, m_ref.shape[2:])


def _ml_scan(lay, q, k, v, gc, gr, heads):
    inner = q.shape[1]
    dh = inner // heads
    ncc, nlc = lay.ctx_len // CHUNK, lay.seq // CHUNK
    ng = gr.shape[0]
    ins = []
    for f in (lay.chunk_fwd, lay.chunk_bwd):
        tok = pl.BlockSpec((CHUNK, inner), lambda b, i, f=f: (f(b, i), 0))
        ins += [tok, pl.BlockSpec((inner, CHUNK), lambda b, i, f=f: (0, f(b, i))), tok]
        ins += [pl.BlockSpec((CHUNK, 128), lambda b, i, f=f: (f(b, i), 0)),
                pl.BlockSpec((ng, CHUNK), lambda b, i, f=f: (0, f(b, i)))]
    outs = tuple(pl.BlockSpec((CHUNK, inner), lambda b, i, f=f: (f(b, i), 0))
                 for f in (lay.chunk_fwd, lay.chunk_bwd))
    return pl.pallas_call(
        functools.partial(_ml_scan_kernel, heads=heads),
        out_shape=(jax.ShapeDtypeStruct((lay.rows, inner), BF16),) * 2,
        grid=(lay.batch, ncc + nlc),
        in_specs=ins,
        out_specs=outs,
        scratch_shapes=[pltpu.VMEM((2, heads, dh, dh + 128), F32),
                        pltpu.VMEM((2, heads, dh, dh + 128), BF16),
                        pltpu.VMEM((2, heads, 8, 128), F32)],
        compiler_params=_cparams(("arbitrary", "arbitrary")),
        name="mlstm_scan",
    )(q, k, v, gc, gr, q, k, v, gc, gr)


def _ml_tail_kernel(x_ref, m_ref, hf_ref, hb_ref, xc_ref, op_ref, gn_ref, sk_ref, wout_ref,
                    g_ref, w1_ref, w2_ref, fg_ref, o_ref, act_ref, *, head_dim, hc):
    i = pl.program_id(0)
    fill, drain = i % 2, 1 - i % 2
    tm = x_ref.shape[0]
    n_chunks = w1_ref.shape[1] // hc
    rc = tm // n_chunks

    @pl.when(i == 0)
    def _():
        act_ref[...] = jnp.zeros(act_ref.shape, act_ref.dtype)

    def prepare(c):
        rs = slice(c * rc, (c + 1) * rc)
        hsum = hf_ref[rs, :].astype(F32) + hb_ref[rs, :].astype(F32)
        hn = _group_norm(hsum, head_dim) * gn_ref[...] + sk_ref[...] * xc_ref[rs, :].astype(F32)
        act_ref[fill, rs, :] = (_sigmoid(op_ref[rs, :].astype(F32)) * hn).astype(act_ref.dtype)

    m = m_ref[0]
    lat = x_ref[...] + m[2:3] * _dot(act_ref[drain], wout_ref[...])
    h = _rms_mod(lat, g_ref[...], m[3:4], m[4:5]).astype(BF16)
    acc = jnp.zeros(lat.shape, F32)
    for c in range(n_chunks):
        a = _dot(h, w1_ref[:, c * hc:(c + 1) * hc])
        a = jnp.square(jnp.maximum(a, 0.0)).astype(BF16)
        acc = acc + _dot(a, w2_ref[c * hc:(c + 1) * hc, :])
        prepare(c)
    y = lat + m[5:6] * acc
    o_ref[...] = y * lax.rsqrt(jnp.mean(y * y, axis=-1, keepdims=True) + EPS) * fg_ref[...]


def _ml_tail(lay, stream, mods, hf, hb, xc, o_pre, gn_g, skip, w_out, g, w1, w2, layer, final_g,
             heads):
    d = stream.shape[1]
    inner = hf.shape[1]
    tm = ROW_TILE
    sk = lay.ctx_rows // tm
    n_tiles = lay.rows // tm - sk
    per_b = lay.seq // tm

    def cur(w):
        return pl.BlockSpec((tm, w), lambda i: (jnp.minimum(i, n_tiles - 1) + sk, 0))

    def prev(i):
        return jnp.maximum(i - 1, 0)

    return pl.pallas_call(
        functools.partial(_ml_tail_kernel, head_dim=inner // heads, hc=512),
        out_shape=jax.ShapeDtypeStruct((n_tiles * tm, d), F32),
        grid=(n_tiles + 1,),
        in_specs=[pl.BlockSpec((tm, d), lambda i: (prev(i) + sk, 0)),
                  pl.BlockSpec((1,) + mods.shape[1:], lambda i: (prev(i) // per_b, 0, 0)),
                  cur(inner), cur(inner), cur(inner), cur(inner),
                  _const_spec((1, inner)), _const_spec((1, inner)), _const_spec(w_out.shape),
                  _const_spec((1, d)), _layer_spec(w1, layer), _layer_spec(w2, layer),
                  _const_spec((1, d))],
        out_specs=pl.BlockSpec((tm, d), lambda i: (prev(i), 0)),
        scratch_shapes=[pltpu.VMEM((2, tm, inner), BF16)],
        compiler_params=_cparams(("arbitrary",)),
        name="mlstm_out_mlp_final",
    )(stream, mods, hf, hb, xc, o_pre, gn_g.reshape(1, inner), skip.reshape(1, inner), w_out,
      g.reshape(1, d), w1, w2, final_g.reshape(1, d))


def kernel(x, c, ctx, c_ctx, mod_w, mod_b, norm_mix_g, norm_mlp_g, mlp_w1, mlp_w2, final_norm_g,
           ev_w_in, ev_w_out, s5_lambda_re, s5_lambda_im, s5_log_step, s5_b_re, s5_b_im, s5_c_re,
           s5_c_im, s5_d, s5_w_glu, s5_b_glu, ret_decay_logit, ret_gn_g,
           ml_w_in, ml_gate_b, ml_conv_w, ml_conv_b, ml_wq, ml_wk, ml_wv, ml_gn_g, ml_skip,
           ml_w_out):
    bsz, seq, d = x.shape
    ctx_len = ctx.shape[1]
    depth = mod_w.shape[0]
    assert depth == 2, "one even (S5 + retention) layer followed by one mLSTM layer"
    lay = _Layout(bsz, ctx_len, seq)
    assert lay.ctx_rows % ROW_TILE == 0 and seq % ROW_TILE == 0
    assert ctx_len % (S5_SEGS * S5_TB) == 0 and seq % (S5_SEGS * S5_TB) == 0
    assert ctx_len % CHUNK == 0 and seq % CHUNK == 0

    c_all = jnp.concatenate([c, c_ctx[None, :], jnp.zeros((8 - bsz - 1, d), F32)], axis=0)
    mods = _mod_table(c_all, mod_w, mod_b)
    ctx2d, x2d = ctx.reshape(bsz * ctx_len, d), x.reshape(bsz * seq, d)

    ret_w = ret_gn_g.shape[1]
    ret_heads = ret_decay_logit.shape[2]
    u_ctx, u_lat, p0 = _proj(lay, ctx2d, x2d, mods[0], norm_mix_g[0], ev_w_in[0].astype(BF16),
                             s5_d.shape[1])
    y_parts = _s5_mixer(lay, u_ctx, u_lat, s5_lambda_re[0], s5_lambda_im[0], s5_log_step[0],
                        s5_b_re[0], s5_b_im[0], s5_c_re[0], s5_c_im[0])
    rf, rb = _retention(lay, p0, ret_decay_logit[0], ret_w)
    stream = _even_out(lay, ctx2d, x2d, mods[0], (u_ctx, u_lat), y_parts, p0, rf, rb, s5_d[0],
                       s5_w_glu[0].astype(BF16), s5_b_glu[0], ret_gn_g[0],
                       ev_w_out[0].astype(BF16), ret_heads)
    w1_all, w2_all = mlp_w1.astype(BF16), mlp_w2.astype(BF16)
    stream = _mlp(lay, stream, mods[0], norm_mlp_g[0], w1_all, w2_all, 0)

    ml_heads = ml_gate_b.shape[1] // 4
    q, k, v, o_pre, xc, gc, gr = _ml_proj(lay, stream, mods[1], norm_mix_g[1], ml_w_in[0],
                                          ml_gate_b[0], ml_conv_w[0], ml_conv_b[0],
                                          ml_wq[0], ml_wk[0], ml_wv[0], ml_heads)
    hf, hb = _ml_scan(lay, q, k, v, gc, gr, ml_heads)
    out = _ml_tail(lay, stream, mods[1], hf, hb, xc, o_pre, ml_gn_g[0], ml_skip[0],
                   ml_w_out[0].astype(BF16), norm_mlp_g[1], w1_all, w2_all, 1, final_norm_g,
                   ml_heads)
    return out.reshape(bsz, seq, d)
```

```python
import functools
import math

import jax
import jax.numpy as jnp
from jax import lax
from jax.experimental import pallas as pl
from jax.experimental.pallas import tpu as pltpu

F32 = jnp.float32
BF16 = jnp.bfloat16

EPS = 1e-6
GRID_W = 64
ROPE_BASE = 10000.0
CHUNK = 256
RET_CHUNK = 256
ML_QKV_BLOCK = 4
V7X_SCOPED_VMEM_BYTES = 60000 * 1024
S5_SEGS = 4
S5_TB = 32
S5_MATMUL_AHEAD = 1
ROW_TILE = 512
ML_ROW_TILE = 256
CONV_HALO = 8
ML_COL_CHUNK = 512
ML_PROJ_AHEAD = 1


def _cparams(sem, vmem_bytes=V7X_SCOPED_VMEM_BYTES):
    return pltpu.CompilerParams(dimension_semantics=sem, vmem_limit_bytes=vmem_bytes)


def _const_spec(shape):
    nd = len(shape)
    return pl.BlockSpec(shape, lambda *_: (0,) * nd, pipeline_mode=pl.Buffered(1))


def _dot(a, b):
    return jnp.dot(a, b, preferred_element_type=F32)


def _dot_nt(a, b):
    return lax.dot_general(a, b, (((1,), (1,)), ((), ())), preferred_element_type=F32)


def _dot_tn(a, b):
    return lax.dot_general(a, b, (((0,), (0,)), ((), ())), preferred_element_type=F32)


def _dot_f32(a, b):
    return jnp.dot(a, b, preferred_element_type=F32, precision=lax.Precision.HIGHEST)


def _sigmoid(x):
    return jax.nn.sigmoid(x)


def _silu(x):
    return x * jax.nn.sigmoid(x)


def _log_sigmoid(x):
    return jnp.minimum(x, 0.0) - jnp.log1p(jnp.exp(-jnp.abs(x)))


def _rms_mod(x, g, shift, scale):
    y = x * lax.rsqrt(jnp.mean(x * x, axis=-1, keepdims=True) + EPS) * g
    return y * (1.0 + scale) + shift


def _group_norm(x, width):
    outs = []
    for h in range(x.shape[-1] // width):
        xh = x[:, h * width:(h + 1) * width]
        xc = xh - jnp.mean(xh, axis=-1, keepdims=True)
        var = jnp.mean(xc * xc, axis=-1, keepdims=True)
        outs.append(xc * lax.rsqrt(var + EPS))
    return jnp.concatenate(outs, axis=-1)


class _Layout:
    def __init__(self, batch, ctx_len, seq):
        self.batch, self.ctx_len, self.seq = batch, ctx_len, seq
        self.ctx_rows = batch * ctx_len
        self.rows = batch * (ctx_len + seq)

    def mod_row(self, tile, tile_rows):
        n_ctx = self.ctx_rows // tile_rows
        return jnp.where(tile < n_ctx, self.batch, (tile - n_ctx) // (self.seq // tile_rows))

    def chunk_fwd(self, b, i, chunk=CHUNK):
        ncc, nlc = self.ctx_len // chunk, self.seq // chunk
        return jnp.where(i < ncc, b * ncc + i, self.batch * ncc + b * nlc + (i - ncc))

    def chunk_bwd(self, b, i, chunk=CHUNK):
        ncc, nlc = self.ctx_len // chunk, self.seq // chunk
        return jnp.where(i < ncc, b * ncc + (ncc - 1 - i),
                         self.batch * ncc + b * nlc + (nlc - 1 - (i - ncc)))


def _mod_kernel(c_ref, w_ref, b_ref, o_ref):
    s = _silu(c_ref[...]).astype(BF16)
    o_ref[0] = _dot(s, w_ref[0].astype(BF16)) + b_ref[0]


def _mod_table(c_all, mod_w, mod_b):
    depth, d, n = mod_w.shape
    tn = 1536
    out = pl.pallas_call(
        _mod_kernel,
        out_shape=jax.ShapeDtypeStruct((depth, c_all.shape[0], n), F32),
        grid=(depth, n // tn),
        in_specs=[pl.BlockSpec(c_all.shape, lambda l, j: (0, 0)),
                  pl.BlockSpec((1, d, tn), lambda l, j: (l, 0, j)),
                  pl.BlockSpec((1, 1, tn), lambda l, j: (l, 0, j))],
        out_specs=pl.BlockSpec((1, c_all.shape[0], tn), lambda l, j: (l, 0, j)),
        compiler_params=_cparams(("arbitrary", "arbitrary")),
        name="mod_table",
    )(c_all, mod_w, mod_b.reshape(depth, 1, n))
    return out.reshape(depth, c_all.shape[0], n // d, d)


def _two_source(lay, tm, width):
    n_ctx = lay.ctx_rows // tm
    return [pl.BlockSpec((tm, width), lambda i: (jnp.minimum(i, n_ctx - 1), 0)),
            pl.BlockSpec((tm, width), lambda i: (jnp.maximum(i - n_ctx, 0), 0))]


def _proj_kernel(xc_ref, xl_ref, m_ref, g_ref, w_ref, uc_ref, ul_ref, qkv_ref, gate_ref,
                 col_ref, tmp_ref, *, n_ctx, tiles_per_seq, heads):
    i = pl.program_id(0)
    tm = xc_ref.shape[0]
    is_ctx = i < n_ctx

    @pl.when(i == 0)
    def _():
        lane = lax.broadcasted_iota(jnp.int32, (tm, 128), 1)
        tok = lax.broadcasted_iota(jnp.int32, (tm, 128), 0)
        inv = jnp.exp((lane & 31).astype(F32) * (-math.log(ROPE_BASE) / 32))
        ang = (tok % GRID_W).astype(F32) * inv
        col_ref[0] = jnp.cos(ang)
        col_ref[1] = jnp.sin(ang)

    m = m_ref[0]
    x = jnp.where(is_ctx, xc_ref[...], xl_ref[...])
    h = _rms_mod(x, g_ref[...], m[0:1], m[1:2]).astype(BF16)
    uw = uc_ref.shape[1]
    u = _dot(h, w_ref[:, 0:uw]).astype(uc_ref.dtype)

    @pl.when(is_ctx)
    def _():
        uc_ref[...] = u

    @pl.when(jnp.logical_not(is_ctx))
    def _():
        ul_ref[...] = u

    rw = gate_ref.shape[1]
    dh = rw // heads
    tile_in_seq = jnp.maximum(i - n_ctx, 0) % tiles_per_seq
    cos_t, sin_t = _rope_tables(tile_in_seq, col_ref[0], col_ref[1], jnp.logical_not(is_ctx))
    def rotate(part):
        for hh in range(heads):
            xh = tmp_ref[part, :, hh * dh:(hh + 1) * dh]
            y = xh * cos_t + pltpu.roll(xh, dh // 2, 1) * sin_t
            if part == 1:
                y = y * dh ** -0.5
            qkv_ref[:, part * rw + hh * dh:part * rw + (hh + 1) * dh] = y.astype(qkv_ref.dtype)

    tmp_ref[0] = _dot(h, w_ref[:, uw:uw + rw])
    tmp_ref[1] = _dot(h, w_ref[:, uw + rw:uw + 2 * rw])
    rotate(0)
    qkv_ref[:, 2 * rw:3 * rw] = _dot(h, w_ref[:, uw + 2 * rw:uw + 3 * rw]).astype(qkv_ref.dtype)
    rotate(1)
    gate_ref[...] = _dot(h, w_ref[:, uw + 3 * rw:uw + 4 * rw])


def _proj(lay, ctx2d, x2d, mods, g, w, uw, heads):
    d = x2d.shape[1]
    rw = (w.shape[1] - uw) // 4
    tm = ROW_TILE
    n_ctx = lay.ctx_rows // tm
    assert rw // heads == 128 and tm % GRID_W == 0
    return pl.pallas_call(
        functools.partial(_proj_kernel, n_ctx=n_ctx, tiles_per_seq=lay.seq // tm, heads=heads),
        out_shape=(jax.ShapeDtypeStruct((lay.ctx_rows, uw), BF16),
                   jax.ShapeDtypeStruct((x2d.shape[0], uw), BF16),
                   jax.ShapeDtypeStruct((lay.rows, 3 * rw), BF16),
                   jax.ShapeDtypeStruct((lay.rows, rw), F32)),
        grid=(lay.rows // tm,),
        in_specs=_two_source(lay, tm, d) + [
            pl.BlockSpec((1,) + mods.shape[1:], lambda i: (lay.mod_row(i, tm), 0, 0)),
            _const_spec((1, d)),
            _const_spec(w.shape)],
        out_specs=tuple(_two_source(lay, tm, uw) + [pl.BlockSpec((tm, 3 * rw), lambda i: (i, 0)),
                                                    pl.BlockSpec((tm, rw), lambda i: (i, 0))]),
        scratch_shapes=[pltpu.VMEM((2, tm, 128), F32), pltpu.VMEM((2, tm, rw), F32)],
        compiler_params=_cparams(("arbitrary",)),
        name="even_in_proj",
    )(ctx2d, x2d, mods, g.reshape(1, d), w)


def _mlp_kernel(x_ref, m_ref, g_ref, w1_ref, w2_ref, o_ref, *, hc):
    x = x_ref[...]
    m = m_ref[0]
    h = _rms_mod(x, g_ref[...], m[3:4], m[4:5]).astype(BF16)
    acc = jnp.zeros(x.shape, F32)
    for c in range(w1_ref.shape[1] // hc):
        a = _dot(h, w1_ref[:, c * hc:(c + 1) * hc])
        a = jnp.square(jnp.maximum(a, 0.0)).astype(BF16)
        acc = acc + _dot(a, w2_ref[c * hc:(c + 1) * hc, :])
    o_ref[...] = x + m[5:6] * acc


def _layer_spec(w, layer):
    nd = w.ndim - 1
    return pl.BlockSpec((None,) + w.shape[1:], lambda *_: (layer,) + (0,) * nd,
                        pipeline_mode=pl.Buffered(1))


def _mlp(lay, stream, mods, g, w1, w2, layer):
    rows, d = stream.shape
    tm = ROW_TILE
    return pl.pallas_call(
        functools.partial(_mlp_kernel, hc=512),
        out_shape=jax.ShapeDtypeStruct((rows, d), F32),
        grid=(rows // tm,),
        in_specs=[pl.BlockSpec((tm, d), lambda i: (i, 0)),
                  pl.BlockSpec((1,) + mods.shape[1:], lambda i: (lay.mod_row(i, tm), 0, 0)),
                  _const_spec((1, d)),
                  _layer_spec(w1, layer),
                  _layer_spec(w2, layer)],
        out_specs=pl.BlockSpec((tm, d), lambda i: (i, 0)),
        compiler_params=_cparams(("arbitrary",)),
        name="mlp",
    )(stream, mods, g.reshape(1, d), w1, w2)


def _s5_disc_kernel(lre_ref, lim_ref, ls_ref, bre_ref, bim_ref,
                    lam_ref, bbar_ref, pow_ref, *, seg_ctx, seg_lat):
    lre, lim = lre_ref[...], lim_ref[...]
    step = jnp.exp(ls_ref[...])
    mag = jnp.exp(lre * step)
    ang = lim * step
    lbr, lbi = mag * jnp.cos(ang), mag * jnp.sin(ang)
    lam_ref[0], lam_ref[1] = lbr, lbi
    den = lre * lre + lim * lim
    nr, ni = lbr - 1.0, lbi
    cr = (nr * lre + ni * lim) / den
    ci = (ni * lre - nr * lim) / den
    bre, bim = bre_ref[...], bim_ref[...]
    bbar_ref[0] = cr[:, None] * bre - ci[:, None] * bim
    bbar_ref[1] = cr[:, None] * bim + ci[:, None] * bre
    for k, n in enumerate((seg_ctx, seg_lat)):
        mag_n = jnp.exp(lre * step * n)
        ang_n = lim * step * n
        pow_ref[2 * k] = mag_n * jnp.cos(ang_n)
        pow_ref[2 * k + 1] = mag_n * jnp.sin(ang_n)


def _s5_discretize(lam_re, lam_im, log_step, b_re, b_im, seg_ctx, seg_lat):
    _, g, p = lam_re.shape
    n = b_re.shape[-1]
    gp = g * p
    lt = gp // 128
    shp = (2, lt, 128)
    ls = jnp.broadcast_to(log_step[:, :, None], (2, g, p)).reshape(shp)
    bre = b_re.reshape(2, gp, n).transpose(0, 2, 1).reshape(2, n, lt, 128)
    bim = b_im.reshape(2, gp, n).transpose(0, 2, 1).reshape(2, n, lt, 128)
    lam, bbar, pw = pl.pallas_call(
        functools.partial(_s5_disc_kernel, seg_ctx=seg_ctx, seg_lat=seg_lat),
        out_shape=(jax.ShapeDtypeStruct((2,) + shp, F32),
                   jax.ShapeDtypeStruct((2, 2, n, lt, 128), F32),
                   jax.ShapeDtypeStruct((4,) + shp, F32)),
        name="s5_discretize",
    )(lam_re.reshape(shp), lam_im.reshape(shp), ls, bre, bim)
    return lam.reshape(2, 2, gp), bbar.reshape(2, 2, n, gp), pw.reshape(4, 2, gp)


def _s5_scan_block(u_refs, wb_ref, wc_ref, lam_ref, st_ref, h_ref, y_refs, *, tb, final_pass):
    nkt, nrow, gw = st_ref.shape[1], st_ref.shape[3], st_ref.shape[4]
    lane_tiles = [slice(c * 128, (c + 1) * 128) for c in range(gw // 128)]

    pieces = [(2 * s + d, d, s) for s in range(nkt) for d in range(2)]

    def input_matmul(d, kt):
        u = u_refs[d, :, kt * 128:(kt + 1) * 128]
        return _dot(u, wb_ref[d, kt, 0]), _dot(u, wb_ref[d, kt, 1])

    def scan(hb, d, kt, bu_re, bu_im):
        lr = [jnp.broadcast_to(lam_ref[d, kt, 0, :, cs], (nrow, 128)) for cs in lane_tiles]
        li = [jnp.broadcast_to(lam_ref[d, kt, 1, :, cs], (nrow, 128)) for cs in lane_tiles]
        hr = [st_ref[d, kt, 0, :, cs] for cs in lane_tiles]
        hi = [st_ref[d, kt, 1, :, cs] for cs in lane_tiles]
        for t in range(tb):
            rows = slice(t * nrow, (t + 1) * nrow)
            br, bi = bu_re[rows, :], bu_im[rows, :]
            for c, cs in enumerate(lane_tiles):
                xr = lr[c] * hr[c] - li[c] * hi[c] + br[:, cs]
                xi = lr[c] * hi[c] + li[c] * hr[c] + bi[:, cs]
                if final_pass:
                    h_ref[hb, 0, rows, cs] = xr
                    h_ref[hb, 1, rows, cs] = xi
                hr[c], hi[c] = xr, xi
        for c, cs in enumerate(lane_tiles):
            st_ref[d, kt, 0, :, cs] = hr[c]
            st_ref[d, kt, 1, :, cs] = hi[c]

    ahead = [input_matmul(d, kt) for _, d, kt in pieces[:S5_MATMUL_AHEAD]]
    for i, (hb, d, kt) in enumerate(pieces):
        if i + S5_MATMUL_AHEAD < len(pieces):
            _, d2, kt2 = pieces[i + S5_MATMUL_AHEAD]
            ahead.append(input_matmul(d2, kt2))
        scan(hb, d, kt, *ahead.pop(0))
        if final_pass:
            y_refs[d, :, kt * 128:(kt + 1) * 128] = (
                _dot(h_ref[hb, 0].astype(BF16), wc_ref[d, kt, 0])
                - _dot(h_ref[hb, 1].astype(BF16), wc_ref[d, kt, 1]))


def _s5_scan_kernel(*refs, tb, ncb, nblk, final_pass):
    if final_pass:
        (ucf_ref, ulf_ref, ucb_ref, ulb_ref, perm_ref, permt_ref, wb_ref, wc_ref, lam_ref,
         pow_ref, fin_ref, ycf_ref, ylf_ref, ycb_ref, ylb_ref,
         st_ref, u_scr, init_ref, h_ref, y_scr) = refs
    else:
        (ucf_ref, ulf_ref, ucb_ref, ulb_ref, perm_ref, wb_ref, lam_ref, fin_ref,
         st_ref, u_scr) = refs
        wc_ref, h_ref, y_scr = None, None, None
    i = pl.program_id(0)
    part = jnp.where(i < ncb, 0, 1)
    first = jnp.logical_or(i == 0, i == ncb)
    last = jnp.logical_or(i == ncb - 1, i == nblk - 1)
    nseg = S5_SEGS
    nkt, nrow = st_ref.shape[1], st_ref.shape[3]

    @pl.when(i == 0)
    def _():
        if final_pass:
            for d in range(2):
                for kt in range(nkt):
                    for b in range(nrow // nseg):
                        cr = jnp.zeros((1, st_ref.shape[4]), F32)
                        ci = jnp.zeros((1, st_ref.shape[4]), F32)
                        for prt in range(2):
                            pr = pow_ref[d, prt, kt, 0]
                            pi = pow_ref[d, prt, kt, 1]
                            for s in range(nseg):
                                seg = s if d == 0 else nseg - 1 - s
                                r = b * nseg + seg
                                init_ref[d, prt, kt, 0, r:r + 1, :] = cr
                                init_ref[d, prt, kt, 1, r:r + 1, :] = ci
                                fr = fin_ref[d, prt, kt, 0, r:r + 1, :]
                                fi = fin_ref[d, prt, kt, 1, r:r + 1, :]
                                cr, ci = pr * cr - pi * ci + fr, pr * ci + pi * cr + fi

    @pl.when(first)
    def _():
        if final_pass:
            for d in range(2):
                st_ref[d] = jnp.where(part == 0, init_ref[d, 0], init_ref[d, 1])
        else:
            st_ref[...] = jnp.zeros(st_ref.shape, F32)

    blk = tb * nrow
    for d, (uc_ref, ul_ref) in enumerate(((ucf_ref, ulf_ref), (ucb_ref, ulb_ref))):
        u_nat = jnp.where(part == 0, uc_ref[...], ul_ref[...]).reshape(blk, uc_ref.shape[2])
        u_scr[d] = _dot(perm_ref[d], u_nat).astype(BF16)

    _s5_scan_block(u_scr, wb_ref, wc_ref, lam_ref, st_ref, h_ref, y_scr,
                   tb=tb, final_pass=final_pass)

    if final_pass:
        for d, (yc_ref, yl_ref) in enumerate(((ycf_ref, ylf_ref), (ycb_ref, ylb_ref))):
            y_nat = _dot(permt_ref[d], y_scr[d].astype(BF16)).astype(BF16).reshape(yc_ref.shape)

            @pl.when(part == 0)
            def _(yc_ref=yc_ref, y_nat=y_nat):
                yc_ref[...] = y_nat

            @pl.when(part == 1)
            def _(yl_ref=yl_ref, y_nat=y_nat):
                yl_ref[...] = y_nat

    if not final_pass:
        @pl.when(last)
        def _():
            for d in range(2):
                for prt in range(2):
                    @pl.when(part == prt)
                    def _(d=d, prt=prt):
                        fin_ref[d, prt] = st_ref[d]


def _s5_scan(u_c, u_l, perm, wb, wc, lam, pw, fin, *, final_pass):
    nkt, nrow, gw = fin.shape[2], fin.shape[4], fin.shape[5]
    uw = u_c.shape[2]
    tb = S5_TB
    ncb, nlb = u_c.shape[1] // tb, u_l.shape[1] // tb
    nblk = ncb + nlb

    def c_f(i):
        return jnp.minimum(i, ncb - 1)

    def l_f(i):
        return jnp.clip(i - ncb, 0, nlb - 1)

    def spec(pos):
        return pl.BlockSpec((nrow, tb, uw), lambda i: (0, pos(i), 0))

    tiles = [spec(c_f), spec(l_f), spec(lambda i: ncb - 1 - c_f(i)), spec(lambda i: nlb - 1 - l_f(i))]
    kern = functools.partial(_s5_scan_kernel, tb=tb, ncb=ncb, nblk=nblk, final_pass=final_pass)
    state = pltpu.VMEM((2, nkt, 2, nrow, gw), F32)
    u_scr = pltpu.VMEM((2, tb * nrow, uw), BF16)
    if final_pass:
        perm_t = perm.transpose(0, 2, 1)
        return pl.pallas_call(
            kern,
            out_shape=(jax.ShapeDtypeStruct(u_c.shape, BF16), jax.ShapeDtypeStruct(u_l.shape, BF16)) * 2,
            grid=(nblk,),
            in_specs=tiles + [_const_spec(perm.shape), _const_spec(perm.shape),
                              _const_spec(wb.shape), _const_spec(wc.shape),
                              _const_spec(lam.shape), _const_spec(pw.shape), _const_spec(fin.shape)],
            out_specs=tuple(tiles),
            scratch_shapes=[state, u_scr, pltpu.VMEM(fin.shape, F32),
                            pltpu.VMEM((2 * nkt, 2, tb * nrow, gw), F32),
                            pltpu.VMEM((2, tb * nrow, uw), F32)],
            compiler_params=_cparams(("arbitrary",)),
            name="s5_scan_out",
        )(u_c, u_l, u_c, u_l, perm, perm_t, wb, wc, lam, pw, fin)
    return pl.pallas_call(
        kern,
        out_shape=jax.ShapeDtypeStruct(fin.shape, F32),
        grid=(nblk,),
        in_specs=tiles + [_const_spec(perm.shape), _const_spec(wb.shape), _const_spec(lam.shape)],
        out_specs=pl.BlockSpec(fin.shape, lambda i: (0,) * len(fin.shape)),
        scratch_shapes=[state, u_scr],
        compiler_params=_cparams(("arbitrary",)),
        name="s5_scan_state",
    )(u_c, u_l, u_c, u_l, perm, wb, lam)


def _s5_mixer(lay, u_ctx, u_lat, lam_re, lam_im, log_step, b_re, b_im, c_re, c_im):
    bsz = lay.batch
    _, g, pst = lam_re.shape
    ngrp = b_re.shape[-1]
    uw = g * ngrp
    half = g * pst
    nseg = S5_SEGS
    seg_ctx, seg_lat = lay.ctx_len // nseg, lay.seq // nseg
    nrow = bsz * nseg

    lam, bbar, pw = _s5_discretize(lam_re, lam_im, log_step, b_re, b_im, seg_ctx, seg_lat)

    kt_g = 128 // ngrp
    kt_n = g // kt_g
    bb = bbar.reshape(2, 2, ngrp, kt_n, kt_g, pst)
    eye = jnp.eye(kt_g, dtype=bool)
    wb = jnp.where(eye[None, None, None, :, None, :, None],
                   bb.transpose(0, 1, 3, 2, 4, 5)[:, :, :, None, :, :, :], 0.0)
    wb = wb.reshape(2, 2, kt_n, kt_g * ngrp, kt_g * pst)
    wb = jnp.stack([wb[0], wb[1]], axis=2).astype(BF16)

    gw = kt_g * pst

    def c_tiles(c):
        cc = c.reshape(2, kt_n, kt_g, ngrp, pst).transpose(0, 1, 2, 4, 3)
        w = jnp.where(eye[None, None, :, None, :, None], cc[:, :, :, :, None, :], 0.0)
        return w.reshape(2, kt_n, gw, kt_g * ngrp)

    wc = jnp.stack([c_tiles(c_re), c_tiles(c_im)], axis=2).astype(BF16)

    def grouped(a, lead):
        a = a.reshape(lead + (2, 2, kt_n, 1, gw))
        perm = (len(lead) + 1,) + tuple(range(len(lead))) + (len(lead) + 2, len(lead), len(lead) + 3,
                                                             len(lead) + 4)
        return a.transpose(perm)

    lam_g = grouped(lam, ())
    pw_g = grouped(pw.reshape(2, 2, 2, half), (2,))

    tb = S5_TB
    out_row = jnp.arange(tb * nrow)[:, None]
    in_row = jnp.arange(tb * nrow)[None, :]
    t_o, r_o = out_row // nrow, out_row % nrow
    perm = jnp.stack([(in_row == r_o * tb + tt).astype(BF16) for tt in (t_o, tb - 1 - t_o)])

    u_c = u_ctx.reshape(nrow, seg_ctx, uw)
    u_l = u_lat.reshape(nrow, seg_lat, uw)
    fin0 = jnp.zeros((2, 2, kt_n, 2, nrow, gw), F32)
    fin = _s5_scan(u_c, u_l, perm, wb, None, lam_g, None, fin0, final_pass=False)
    ycf, ylf, ycb, ylb = _s5_scan(u_c, u_l, perm, wb, wc, lam_g, pw_g, fin, final_pass=True)
    flat = lambda y: y.reshape(-1, uw)
    return flat(ycf), flat(ylf), flat(ycb), flat(ylb)


def _rope_tables(chunk_idx, col_cos, col_sin, is_lat):
    chunk = col_cos.shape[0]
    lane = lax.broadcasted_iota(jnp.int32, (8, 128), 1)
    sub = lax.broadcasted_iota(jnp.int32, (8, 128), 0)
    quarter = 32
    inv = jnp.exp((lane & (quarter - 1)).astype(F32) * (-math.log(ROPE_BASE) / quarter))
    rows_per_chunk = chunk // GRID_W
    ang = (chunk_idx * rows_per_chunk + sub).astype(F32) * inv
    rc, rs = jnp.cos(ang), jnp.sin(ang)
    tok = lax.broadcasted_iota(jnp.int32, (chunk, 128), 0)
    lane_t = lax.broadcasted_iota(jnp.int32, (chunk, 128), 1)
    row_cos = jnp.zeros((chunk, 128), F32)
    row_sin = jnp.zeros((chunk, 128), F32)
    for r in range(rows_per_chunk):
        sel = (tok // GRID_W) == r
        row_cos = jnp.where(sel, rc[r:r + 1, :], row_cos)
        row_sin = jnp.where(sel, rs[r:r + 1, :], row_sin)
    is_col = (lane_t & quarter) != 0
    cos_t = jnp.where(is_col, col_cos, row_cos)
    sin_t = jnp.where(is_col, col_sin, row_sin)
    sin_t = jnp.where(lane_t < 64, -sin_t, sin_t)
    cos_t = jnp.where(is_lat, cos_t, 1.0)
    sin_t = jnp.where(is_lat, sin_t, 0.0)
    return cos_t, sin_t


def _ret_kernel(dl_ref, qf_ref, kf_ref, vf_ref, qb_ref, kb_ref, vb_ref, of_ref, ob_ref,
                st_ref, intra_ref, tab_ref, *, heads):
    i = pl.program_id(1)
    dh = qf_ref.shape[1] // heads
    chunk = qf_ref.shape[0]

    @pl.when(i == 0)
    def _():
        st_ref[...] = jnp.zeros(st_ref.shape, F32)
        ii = lax.broadcasted_iota(jnp.int32, (chunk, chunk), 0)
        jj = lax.broadcasted_iota(jnp.int32, (chunk, chunk), 1)
        relf = (ii - jj).astype(F32)
        rowf = lax.broadcasted_iota(jnp.int32, (chunk, dh), 0).astype(F32)
        for d in range(2):
            for h in range(heads):
                lg = _log_sigmoid(jnp.full((chunk, chunk), dl_ref[d, h], F32))
                lgv = _log_sigmoid(jnp.full((chunk, dh), dl_ref[d, h], F32))
                if d == 0:
                    mask = ii >= jj
                    intra = jnp.where(mask, jnp.exp(jnp.where(mask, relf, 0.0) * lg), 0.0)
                    dq = jnp.exp((rowf + 1.0) * lgv)
                    dk = jnp.exp((chunk - 1.0 - rowf) * lgv)
                else:
                    mask = jj > ii
                    intra = jnp.where(mask, jnp.exp(jnp.where(mask, -relf, 0.0) * lg), 0.0)
                    dq = jnp.exp((chunk - rowf) * lgv)
                    dk = jnp.exp(rowf * lgv)
                intra_ref[d, h] = intra
                tab_ref[d, h, 0] = dq
                tab_ref[d, h, 1] = dk
                tab_ref[d, h, 2] = jnp.exp(chunk * lgv)

    for d, (q_ref, k_ref, v_ref, o_ref) in enumerate(((qf_ref, kf_ref, vf_ref, of_ref),
                                                      (qb_ref, kb_ref, vb_ref, ob_ref))):
        for h in range(heads):
            q = q_ref[:, h * dh:(h + 1) * dh]
            k = k_ref[:, h * dh:(h + 1) * dh]
            v = v_ref[:, h * dh:(h + 1) * dh]
            state = st_ref[d, h]
            s = _dot_nt(q, k) * intra_ref[d, h]
            o = _dot(s.astype(BF16), v) + _dot(q, state.astype(BF16)) * tab_ref[d, h, 0]
            o_ref[:, h * dh:(h + 1) * dh] = o.astype(o_ref.dtype)
            kdt = (k.astype(F32) * tab_ref[d, h, 1]).T.astype(BF16)
            st_ref[d, h] = state * tab_ref[d, h, 2, 0:dh, :] + _dot(kdt, v)


def _retention(lay, qkv, decay_logit):
    heads = decay_logit.shape[1]
    width = qkv.shape[1] // 3
    dh = width // heads
    chunk = RET_CHUNK
    assert chunk >= dh and lay.ctx_len % chunk == 0 and lay.seq % chunk == 0
    ncc, nlc = lay.ctx_len // chunk, lay.seq // chunk
    orders = [functools.partial(f, chunk=chunk) for f in (lay.chunk_fwd, lay.chunk_bwd)]

    def spec(col, idx):
        return pl.BlockSpec((chunk, width), lambda b, i: (idx(b, i), col))

    ins = [spec(c, f) for f in orders for c in range(3)]
    outs = tuple(pl.BlockSpec((chunk, width), lambda b, i, f=f: (f(b, i), 0)) for f in orders)
    return pl.pallas_call(
        functools.partial(_ret_kernel, heads=heads),
        out_shape=(jax.ShapeDtypeStruct((lay.rows, width), BF16),) * 2,
        grid=(lay.batch, ncc + nlc),
        in_specs=[pl.BlockSpec(memory_space=pltpu.SMEM)] + ins,
        out_specs=outs,
        scratch_shapes=[pltpu.VMEM((2, heads, dh, dh), F32),
                        pltpu.VMEM((2, heads, chunk, chunk), F32),
                        pltpu.VMEM((2, heads, 3, chunk, dh), F32)],
        compiler_params=_cparams(("arbitrary", "arbitrary")),
        name="retention",
    )(decay_logit, qkv, qkv, qkv, qkv, qkv, qkv)


def _even_out_kernel(xc_ref, xl_ref, uc_ref, ul_ref, yfc_ref, yfl_ref, ybc_ref, ybl_ref, m_ref,
                     rf_ref, rb_ref, g_ref, d_ref, wglu_ref, bglu_ref, gn_ref, wout_ref, o_ref,
                     *, n_ctx, head_dim):
    is_ctx = pl.program_id(0) < n_ctx

    def pick(c_ref, l_ref):
        return jnp.where(is_ctx, c_ref[...], l_ref[...])

    m = m_ref[0]
    uw = uc_ref.shape[1]
    x = pick(xc_ref, xl_ref)
    s5 = (pick(uc_ref, ul_ref).astype(F32) * d_ref[...]
          + pick(yfc_ref, yfl_ref).astype(F32) + pick(ybc_ref, ybl_ref).astype(F32))
    ab = _dot(jax.nn.gelu(s5).astype(BF16), wglu_ref[...]) + bglu_ref[...]
    s5_o = ab[:, :uw] * _sigmoid(ab[:, uw:])
    ret = _group_norm(rf_ref[...].astype(F32) + rb_ref[...].astype(F32), head_dim) * gn_ref[...]
    ret_o = ret * _silu(g_ref[...])
    y = _dot(s5_o.astype(BF16), wout_ref[0:uw, :]) + _dot(ret_o.astype(BF16), wout_ref[uw:, :])
    o_ref[...] = x + m[2:3] * y


def _even_out(lay, ctx2d, x2d, mods, u_parts, y_parts, gate, rf, rb, s5_d, w_glu, b_glu, gn_g,
              w_out, heads):
    d = x2d.shape[1]
    uw = u_parts[0].shape[1]
    rw = rf.shape[1]
    tm = ROW_TILE

    def row(w, col=0):
        return pl.BlockSpec((tm, w), lambda i: (i, col))

    return pl.pallas_call(
        functools.partial(_even_out_kernel, n_ctx=lay.ctx_rows // tm, head_dim=rw // heads),
        out_shape=jax.ShapeDtypeStruct((lay.rows, d), F32),
        grid=(lay.rows // tm,),
        in_specs=(_two_source(lay, tm, d) + _two_source(lay, tm, uw) * 3
                  + [pl.BlockSpec((1,) + mods.shape[1:], lambda i: (lay.mod_row(i, tm), 0, 0)),
                     row(rw), row(rw), row(rw),
                     _const_spec((1, uw)), _const_spec(w_glu.shape), _const_spec((1, 2 * uw)),
                     _const_spec((1, rw)), _const_spec(w_out.shape)]),
        out_specs=row(d),
        compiler_params=_cparams(("arbitrary",)),
        name="even_out",
    )(ctx2d, x2d, *u_parts, *y_parts, mods, rf, rb, gate, s5_d.reshape(1, uw), w_glu,
      b_glu.reshape(1, 2 * uw), gn_g.reshape(1, rw), w_out)


def _ml_proj_kernel(x_ref, xp_ref, xn_ref, m_ref, g_ref, wx_ref, wo_ref, wgt_ref,
                    gbt_ref, cw_ref, cb_ref, wq_ref, wk_ref, wv_ref,
                    q_ref, k_ref, v_ref, op_ref, xc_ref, gc_ref, gr_ref, xm_scr,
                    *, heads, first_last, k_scale):
    i = pl.program_id(0)
    tm = x_ref.shape[0]
    halo = xp_ref.shape[0]
    m = m_ref[0]
    first, last = first_last(i)
    g = g_ref[...]
    hf = _rms_mod(x_ref[...], g, m[0:1], m[1:2])
    hp = _rms_mod(xp_ref[...], g, m[0:1], m[1:2]) * jnp.where(first, 0.0, 1.0)
    hn = _rms_mod(xn_ref[...], g, m[0:1], m[1:2]) * jnp.where(last, 0.0, 1.0)
    h = hf.astype(BF16)
    h_ext = jnp.concatenate([hp, hf, hn], axis=0).astype(BF16)

    def gate_stage():
        gates = _dot_nt(wgt_ref[...], h) + gbt_ref[...]
        si = lax.broadcasted_iota(jnp.int32, (tm, tm), 0)
        ti = lax.broadcasted_iota(jnp.int32, (tm, tm), 1)
        same = (si // CHUNK) == (ti // CHUNK)
        upto = jnp.where(jnp.logical_and(same, si <= ti), 1.0, 0.0)
        from_ = jnp.where(jnp.logical_and(same, si >= ti), 1.0, 0.0)
        kind = lax.broadcasted_iota(jnp.int32, gates.shape, 0) // heads
        ls = _log_sigmoid(gates)
        cum_f = _dot_f32(ls, upto)
        cum_b = _dot_f32(ls, from_)
        gr = jnp.where(kind == 1, cum_f, jnp.where(kind == 3, cum_b, gates))
        gr_ref[...] = gr
        pad = jnp.zeros((gc_ref.shape[1] - gr.shape[0], tm), F32)
        gc_ref[...] = jnp.concatenate([gr, pad], axis=0).T

    width = wx_ref.shape[1]
    cn = xm_scr.shape[2]
    taps = cw_ref.shape[0]

    def project(c):
        cs = slice(c * cn, (c + 1) * cn)
        op_ref[:, cs] = _dot(h, wo_ref[:, cs]).astype(op_ref.dtype)
        xm_scr[c] = _dot(h_ext, wx_ref[:, cs])

    gate_stage()
    for c in range(ML_PROJ_AHEAD):
        project(c)
    for c in range(width // cn):
        cs = slice(c * cn, (c + 1) * cn)
        if c + ML_PROJ_AHEAD < width // cn:
            project(c + ML_PROJ_AHEAD)
        xm = xm_scr[c, halo:halo + tm, :]
        acc = jnp.zeros((tm, cn), F32) + cb_ref[:, cs]
        for t in range(taps):
            off = halo + t - taps // 2
            acc = acc + xm_scr[c, off:off + tm, :] * cw_ref[t:t + 1, cs]
        xc = _silu(acc)
        xc_ref[:, cs] = xc.astype(xc_ref.dtype)
        xcb, xmb = xc.astype(BF16), xm.astype(BF16)
        bw = wq_ref.shape[1]
        for j in range(cn // bw):
            blk = c * (cn // bw) + j
            ls_ = slice(j * bw, (j + 1) * bw)
            os_ = slice(c * cn + j * bw, c * cn + (j + 1) * bw)
            q_ref[:, os_] = _dot(xcb[:, ls_], wq_ref[blk]).astype(BF16)
            k_ref[os_, :] = (_dot_nt(wk_ref[blk], xcb[:, ls_]) * k_scale).astype(BF16)
            v_ref[:, os_] = _dot(xmb[:, ls_], wv_ref[blk]).astype(BF16)


def _blockdiag_tiles(w, tile):
    nb, blk, _ = w.shape
    per = tile // blk
    rows = jnp.tile(w.reshape(nb // per, tile, blk), (1, 1, per))
    a_of_row = jnp.arange(tile)[:, None] // blk
    b_of_lane = jnp.arange(tile)[None, :] // blk
    return jnp.where(a_of_row == b_of_lane, rows, 0.0).astype(BF16)


def _ml_proj(lay, stream, mods, g, w_in, gate_b, conv_w, conv_b, wq, wk, wv, heads):
    rows, d = stream.shape
    inner = conv_w.shape[1]
    ng = 4 * heads
    tm = ML_ROW_TILE
    halo = CONV_HALO
    assert lay.ctx_len % tm == 0 and lay.seq % tm == 0 and tm % CHUNK == 0
    n_ctx = lay.ctx_rows // tm
    ctx_per, lat_per = lay.ctx_len // tm, lay.seq // tm

    def first_last(i):
        j = jnp.where(i < n_ctx, i % ctx_per, (i - n_ctx) % lat_per)
        per = jnp.where(i < n_ctx, ctx_per, lat_per)
        return j == 0, j == per - 1

    w_all = w_in.astype(BF16)
    w_gt = w_in[:, 2 * inner:].T.astype(BF16)
    gb_r = gate_b.reshape(ng, 1)
    tile = 256
    wq_t, wk_t, wv_t = (_blockdiag_tiles(w, tile) for w in (wq, wk, wv))
    wk_t = wk_t.transpose(0, 2, 1)
    r8 = tm // halo
    last8 = rows // halo - 1

    tok = lambda w, dt: jax.ShapeDtypeStruct((rows, w), dt)
    return pl.pallas_call(
        functools.partial(_ml_proj_kernel, heads=heads, first_last=first_last,
                          k_scale=(inner // heads) ** -0.5),
        out_shape=(tok(inner, BF16), jax.ShapeDtypeStruct((inner, rows), BF16), tok(inner, BF16),
                   tok(inner, BF16), tok(inner, BF16), tok(128, F32),
                   jax.ShapeDtypeStruct((ng, rows), F32)),
        grid=(rows // tm,),
        in_specs=[pl.BlockSpec((tm, d), lambda i: (i, 0)),
                  pl.BlockSpec((halo, d), lambda i: (jnp.maximum(i * r8 - 1, 0), 0)),
                  pl.BlockSpec((halo, d), lambda i: (jnp.minimum((i + 1) * r8, last8), 0)),
                  pl.BlockSpec((1,) + mods.shape[1:], lambda i: (lay.mod_row(i, tm), 0, 0)),
                  _const_spec((1, d)),
                  pl.BlockSpec((d, inner), lambda i: (0, 0), pipeline_mode=pl.Buffered(1)),
                  pl.BlockSpec((d, inner), lambda i: (0, 1), pipeline_mode=pl.Buffered(1)),
                  _const_spec(w_gt.shape),
                  _const_spec(gb_r.shape), _const_spec(conv_w.shape), _const_spec((1, inner)),
                  _const_spec(wq_t.shape), _const_spec(wk_t.shape), _const_spec(wv_t.shape)],
        out_specs=tuple([pl.BlockSpec((tm, inner), lambda i: (i, 0)),
                         pl.BlockSpec((inner, tm), lambda i: (0, i))]
                        + [pl.BlockSpec((tm, inner), lambda i: (i, 0))] * 3
                        + [pl.BlockSpec((tm, 128), lambda i: (i, 0)),
                           pl.BlockSpec((ng, tm), lambda i: (0, i))]),
        scratch_shapes=[pltpu.VMEM((inner // ML_COL_CHUNK, tm + 2 * halo, ML_COL_CHUNK), F32)],
        compiler_params=_cparams(("arbitrary",)),
        name="mlstm_in_proj",
    )(stream, stream, stream, mods, g.reshape(1, d), w_all, w_all, w_gt, gb_r,
      conv_w, conv_b.reshape(1, inner), wq_t, wk_t, wv_t)


def _ml_scan_kernel(qf_ref, kf_ref, vf_ref, gcf_ref, grf_ref, qb_ref, kb_ref, vb_ref, gcb_ref,
                    grb_ref, hf_ref, hb_ref, c_ref, cb_ref, m_ref, *, heads):
    i = pl.program_id(1)
    dh = qf_ref.shape[1] // heads
    cols = c_ref.shape[3]
    cw = 256

    @pl.when(i == 0)
    def _():
        c_ref[...] = jnp.zeros(c_ref.shape, F32)
        cb_ref[...] = jnp.zeros(cb_ref.shape, BF16)
        m_ref[...] = jnp.zeros(m_ref.shape, F32)

    ii = lax.broadcasted_iota(jnp.int32, (CHUNK, CHUNK), 0)
    jj = lax.broadcasted_iota(jnp.int32, (CHUNK, CHUNK), 1)
    lane0 = lax.broadcasted_iota(jnp.int32, (CHUNK, cols - dh), 1) == 0
    one_col = jnp.where(lane0, 1.0, 0.0).astype(BF16)
    tok_r = lax.broadcasted_iota(jnp.int32, (1, CHUNK), 1)
    dir_refs = ((qf_ref, kf_ref, vf_ref, gcf_ref, grf_ref, hf_ref),
                (qb_ref, kb_ref, vb_ref, gcb_ref, grb_ref, hb_ref))
    for h in range(heads):
        for d, (q_ref, k_ref, v_ref, gc_ref, gr_ref, h_ref) in enumerate(dir_refs):
            mask = (jj <= ii) if d == 0 else (jj >= ii)
            end = CHUNK - 1 if d == 0 else 0
            hs = slice(h * dh, (h + 1) * dh)
            ki, kb = (2 * d) * heads + h, (2 * d + 1) * heads + h
            q, kt, v = q_ref[:, hs], k_ref[hs, :], v_ref[:, hs]
            b_c = gc_ref[:, kb:kb + 1]
            ig_r = gr_ref[ki:ki + 1, :]
            b_r = gr_ref[kb:kb + 1, :]
            m_prev = m_ref[d, h, 0:1, 0:1]
            log_w = jnp.where(mask, b_c - b_r + ig_r, -jnp.inf)
            log_prev = b_c + m_prev
            m_row = jnp.maximum(log_prev, jnp.max(log_w, axis=-1, keepdims=True))
            w = jnp.exp(log_w - m_row)
            w_prev = jnp.exp(log_prev - m_row)
            s = _dot(q, kt) * w
            qn = _dot(q, cb_ref[d, h, :, dh:cols])
            qn = jnp.sum(jnp.where(lane0, qn, 0.0), axis=-1, keepdims=True)
            num = _dot(s.astype(BF16), v) + w_prev * _dot(q, cb_ref[d, h, :, 0:dh])
            den = jnp.sum(s, axis=-1, keepdims=True) + w_prev * qn
            h_ref[:, hs] = (num / jnp.maximum(jnp.abs(den), jnp.exp(-m_row))).astype(h_ref.dtype)
            b_last = jnp.sum(jnp.where(tok_r == end, b_r, 0.0), axis=-1, keepdims=True)
            log_k = b_last - b_r + ig_r
            m_new = jnp.maximum(b_last + m_prev, jnp.max(log_k, axis=-1, keepdims=True))
            w_k = jnp.exp(log_k - m_new)
            w_c = jnp.exp(b_last + m_prev - m_new)
            kwt = (kt.astype(F32) * w_k).astype(BF16)
            for c0 in range(0, cols, cw):
                c1 = min(c0 + cw, cols)
                rhs = v_ref[:, h * dh + c0:h * dh + c1] if c1 <= dh else one_col
                cnew = w_c * c_ref[d, h, :, c0:c1] + _dot(kwt, rhs)
                c_ref[d, h, :, c0:c1] = cnew
                cb_ref[d, h, :, c0:c1] = cnew.astype(BF16)
            m_ref[d, h] = jnp.broadcast_to(m_new, m_ref.shape[2:])


def _ml_scan(lay, q, k, v, gc, gr, heads):
    inner = q.shape[1]
    dh = inner // heads
    ncc, nlc = lay.ctx_len // CHUNK, lay.seq // CHUNK
    ng = gr.shape[0]
    ins = []
    for f in (lay.chunk_fwd, lay.chunk_bwd):
        tok = pl.BlockSpec((CHUNK, inner), lambda b, i, f=f: (f(b, i), 0))
        ins += [tok, pl.BlockSpec((inner, CHUNK), lambda b, i, f=f: (0, f(b, i))), tok]
        ins += [pl.BlockSpec((CHUNK, 128), lambda b, i, f=f: (f(b, i), 0)),
                pl.BlockSpec((ng, CHUNK), lambda b, i, f=f: (0, f(b, i)))]
    outs = tuple(pl.BlockSpec((CHUNK, inner), lambda b, i, f=f: (f(b, i), 0))
                 for f in (lay.chunk_fwd, lay.chunk_bwd))
    return pl.pallas_call(
        functools.partial(_ml_scan_kernel, heads=heads),
        out_shape=(jax.ShapeDtypeStruct((lay.rows, inner), BF16),) * 2,
        grid=(lay.batch, ncc + nlc),
        in_specs=ins,
        out_specs=outs,
        scratch_shapes=[pltpu.VMEM((2, heads, dh, dh + 128), F32),
                        pltpu.VMEM((2, heads, dh, dh + 128), BF16),
                        pltpu.VMEM((2, heads, 8, 128), F32)],
        compiler_params=_cparams(("arbitrary", "arbitrary")),
        name="mlstm_scan",
    )(q, k, v, gc, gr, q, k, v, gc, gr)


def _ml_tail_kernel(x_ref, m_ref, hf_ref, hb_ref, xc_ref, op_ref, gn_ref, sk_ref, wout_ref,
                    g_ref, w1_ref, w2_ref, fg_ref, o_ref, act_ref, *, head_dim, hc):
    i = pl.program_id(0)
    fill, drain = i % 2, 1 - i % 2
    tm = x_ref.shape[0]
    n_chunks = w1_ref.shape[1] // hc
    rc = tm // n_chunks

    @pl.when(i == 0)
    def _():
        act_ref[...] = jnp.zeros(act_ref.shape, act_ref.dtype)

    def prepare(c):
        rs = slice(c * rc, (c + 1) * rc)
        hsum = hf_ref[rs, :].astype(F32) + hb_ref[rs, :].astype(F32)
        hn = _group_norm(hsum, head_dim) * gn_ref[...] + sk_ref[...] * xc_ref[rs, :].astype(F32)
        act_ref[fill, rs, :] = (_sigmoid(op_ref[rs, :].astype(F32)) * hn).astype(act_ref.dtype)

    m = m_ref[0]
    lat = x_ref[...] + m[2:3] * _dot(act_ref[drain], wout_ref[...])
    h = _rms_mod(lat, g_ref[...], m[3:4], m[4:5]).astype(BF16)
    acc = jnp.zeros(lat.shape, F32)
    for c in range(n_chunks):
        a = _dot(h, w1_ref[:, c * hc:(c + 1) * hc])
        a = jnp.square(jnp.maximum(a, 0.0)).astype(BF16)
        acc = acc + _dot(a, w2_ref[c * hc:(c + 1) * hc, :])
        prepare(c)
    y = lat + m[5:6] * acc
    o_ref[...] = y * lax.rsqrt(jnp.mean(y * y, axis=-1, keepdims=True) + EPS) * fg_ref[...]


def _ml_tail(lay, stream, mods, hf, hb, xc, o_pre, gn_g, skip, w_out, g, w1, w2, layer, final_g,
             heads):
    d = stream.shape[1]
    inner = hf.shape[1]
    tm = ROW_TILE
    sk = lay.ctx_rows // tm
    n_tiles = lay.rows // tm - sk
    per_b = lay.seq // tm

    def cur(w):
        return pl.BlockSpec((tm, w), lambda i: (jnp.minimum(i, n_tiles - 1) + sk, 0))

    def prev(i):
        return jnp.maximum(i - 1, 0)

    return pl.pallas_call(
        functools.partial(_ml_tail_kernel, head_dim=inner // heads, hc=512),
        out_shape=jax.ShapeDtypeStruct((n_tiles * tm, d), F32),
        grid=(n_tiles + 1,),
        in_specs=[pl.BlockSpec((tm, d), lambda i: (prev(i) + sk, 0)),
                  pl.BlockSpec((1,) + mods.shape[1:], lambda i: (prev(i) // per_b, 0, 0)),
                  cur(inner), cur(inner), cur(inner), cur(inner),
                  _const_spec((1, inner)), _const_spec((1, inner)), _const_spec(w_out.shape),
                  _const_spec((1, d)), _layer_spec(w1, layer), _layer_spec(w2, layer),
                  _const_spec((1, d))],
        out_specs=pl.BlockSpec((tm, d), lambda i: (prev(i), 0)),
        scratch_shapes=[pltpu.VMEM((2, tm, inner), BF16)],
        compiler_params=_cparams(("arbitrary",)),
        name="mlstm_out_mlp_final",
    )(stream, mods, hf, hb, xc, o_pre, gn_g.reshape(1, inner), skip.reshape(1, inner), w_out,
      g.reshape(1, d), w1, w2, final_g.reshape(1, d))


def kernel(x, c, ctx, c_ctx, mod_w, mod_b, norm_mix_g, norm_mlp_g, mlp_w1, mlp_w2, final_norm_g,
           ev_w_in, ev_w_out, s5_lambda_re, s5_lambda_im, s5_log_step, s5_b_re, s5_b_im, s5_c_re,
           s5_c_im, s5_d, s5_w_glu, s5_b_glu, ret_decay_logit, ret_gn_g,
           ml_w_in, ml_gate_b, ml_conv_w, ml_conv_b, ml_wq, ml_wk, ml_wv, ml_gn_g, ml_skip,
           ml_w_out):
    bsz, seq, d = x.shape
    ctx_len = ctx.shape[1]
    depth = mod_w.shape[0]
    assert depth == 2, "one even (S5 + retention) layer followed by one mLSTM layer"
    lay = _Layout(bsz, ctx_len, seq)
    assert lay.ctx_rows % ROW_TILE == 0 and seq % ROW_TILE == 0
    assert ctx_len % (S5_SEGS * S5_TB) == 0 and seq % (S5_SEGS * S5_TB) == 0
    assert ctx_len % CHUNK == 0 and seq % CHUNK == 0

    c_all = jnp.concatenate([c, c_ctx[None, :], jnp.zeros((8 - bsz - 1, d), F32)], axis=0)
    mods = _mod_table(c_all, mod_w, mod_b)
    ctx2d, x2d = ctx.reshape(bsz * ctx_len, d), x.reshape(bsz * seq, d)

    ret_w = ret_gn_g.shape[1]
    ret_heads = ret_decay_logit.shape[2]
    u_ctx, u_lat, qkv, gate = _proj(lay, ctx2d, x2d, mods[0], norm_mix_g[0],
                                    ev_w_in[0].astype(BF16), s5_d.shape[1], ret_heads)
    y_parts = _s5_mixer(lay, u_ctx, u_lat, s5_lambda_re[0], s5_lambda_im[0], s5_log_step[0],
                        s5_b_re[0], s5_b_im[0], s5_c_re[0], s5_c_im[0])
    rf, rb = _retention(lay, qkv, ret_decay_logit[0])
    stream = _even_out(lay, ctx2d, x2d, mods[0], (u_ctx, u_lat), y_parts, gate, rf, rb, s5_d[0],
                       s5_w_glu[0].astype(BF16), s5_b_glu[0], ret_gn_g[0],
                       ev_w_out[0].astype(BF16), ret_heads)
    w1_all, w2_all = mlp_w1.astype(BF16), mlp_w2.astype(BF16)
    stream = _mlp(lay, stream, mods[0], norm_mlp_g[0], w1_all, w2_all, 0)

    ml_heads = ml_gate_b.shape[1] // 4
    q, k, v, o_pre, xc, gc, gr = _ml_proj(lay, stream, mods[1], norm_mix_g[1], ml_w_in[0],
                                          ml_gate_b[0], ml_conv_w[0], ml_conv_b[0],
                                          ml_wq[0], ml_wk[0], ml_wv[0], ml_heads)
    hf, hb = _ml_scan(lay, q, k, v, gc, gr, ml_heads)
    out = _ml_tail(lay, stream, mods[1], hf, hb, xc, o_pre, ml_gn_g[0], ml_skip[0],
                   ml_w_out[0].astype(BF16), norm_mlp_g[1], w1_all, w2_all, 1, final_norm_g,
                   ml_heads)
    return out.reshape(bsz, seq, d)
```

```python
import functools
import math

import jax
import jax.numpy as jnp
from jax import lax
from jax.experimental import pallas as pl
from jax.experimental.pallas import tpu as pltpu

F32 = jnp.float32
BF16 = jnp.bfloat16

EPS = 1e-6
GRID_W = 64
ROPE_BASE = 10000.0
CHUNK = 256
RET_CHUNK = 256
ML_QKV_BLOCK = 4
V7X_SCOPED_VMEM_BYTES = 60000 * 1024
S5_SEGS = 4
S5_TB = 32
S5_MATMUL_AHEAD = 1
ROW_TILE = 512
ML_ROW_TILE = 256
CONV_HALO = 8
ML_COL_CHUNK = 512
ML_PROJ_AHEAD = 1


def _cparams(sem, vmem_bytes=V7X_SCOPED_VMEM_BYTES):
    return pltpu.CompilerParams(dimension_semantics=sem, vmem_limit_bytes=vmem_bytes)


def _const_spec(shape):
    nd = len(shape)
    return pl.BlockSpec(shape, lambda *_: (0,) * nd, pipeline_mode=pl.Buffered(1))


def _dot(a, b):
    return jnp.dot(a, b, preferred_element_type=F32)


def _dot_nt(a, b):
    return lax.dot_general(a, b, (((1,), (1,)), ((), ())), preferred_element_type=F32)


def _dot_tn(a, b):
    return lax.dot_general(a, b, (((0,), (0,)), ((), ())), preferred_element_type=F32)


def _dot_f32(a, b):
    return jnp.dot(a, b, preferred_element_type=F32, precision=lax.Precision.HIGHEST)


def _sigmoid(x):
    return jax.nn.sigmoid(x)


def _silu(x):
    return x * jax.nn.sigmoid(x)


def _log_sigmoid(x):
    return jnp.minimum(x, 0.0) - jnp.log1p(jnp.exp(-jnp.abs(x)))


def _rms_mod(x, g, shift, scale):
    y = x * lax.rsqrt(jnp.mean(x * x, axis=-1, keepdims=True) + EPS) * g
    return y * (1.0 + scale) + shift


def _group_norm(x, width):
    outs = []
    for h in range(x.shape[-1] // width):
        xh = x[:, h * width:(h + 1) * width]
        xc = xh - jnp.mean(xh, axis=-1, keepdims=True)
        var = jnp.mean(xc * xc, axis=-1, keepdims=True)
        outs.append(xc * lax.rsqrt(var + EPS))
    return jnp.concatenate(outs, axis=-1)


class _Layout:
    def __init__(self, batch, ctx_len, seq):
        self.batch, self.ctx_len, self.seq = batch, ctx_len, seq
        self.ctx_rows = batch * ctx_len
        self.rows = batch * (ctx_len + seq)

    def mod_row(self, tile, tile_rows):
        n_ctx = self.ctx_rows // tile_rows
        return jnp.where(tile < n_ctx, self.batch, (tile - n_ctx) // (self.seq // tile_rows))

    def chunk_fwd(self, b, i, chunk=CHUNK):
        ncc, nlc = self.ctx_len // chunk, self.seq // chunk
        return jnp.where(i < ncc, b * ncc + i, self.batch * ncc + b * nlc + (i - ncc))

    def chunk_bwd(self, b, i, chunk=CHUNK):
        ncc, nlc = self.ctx_len // chunk, self.seq // chunk
        return jnp.where(i < ncc, b * ncc + (ncc - 1 - i),
                         self.batch * ncc + b * nlc + (nlc - 1 - (i - ncc)))


def _mod_kernel(c_ref, w_ref, b_ref, o_ref):
    s = _silu(c_ref[...]).astype(BF16)
    o_ref[0] = _dot(s, w_ref[0].astype(BF16)) + b_ref[0]


def _mod_table(c_all, mod_w, mod_b):
    depth, d, n = mod_w.shape
    tn = 1536
    out = pl.pallas_call(
        _mod_kernel,
        out_shape=jax.ShapeDtypeStruct((depth, c_all.shape[0], n), F32),
        grid=(depth, n // tn),
        in_specs=[pl.BlockSpec(c_all.shape, lambda l, j: (0, 0)),
                  pl.BlockSpec((1, d, tn), lambda l, j: (l, 0, j)),
                  pl.BlockSpec((1, 1, tn), lambda l, j: (l, 0, j))],
        out_specs=pl.BlockSpec((1, c_all.shape[0], tn), lambda l, j: (l, 0, j)),
        compiler_params=_cparams(("arbitrary", "arbitrary")),
        name="mod_table",
    )(c_all, mod_w, mod_b.reshape(depth, 1, n))
    return out.reshape(depth, c_all.shape[0], n // d, d)


def _two_source(lay, tm, width):
    n_ctx = lay.ctx_rows // tm
    return [pl.BlockSpec((tm, width), lambda i: (jnp.minimum(i, n_ctx - 1), 0)),
            pl.BlockSpec((tm, width), lambda i: (jnp.maximum(i - n_ctx, 0), 0))]


def _proj_kernel(xc_ref, xl_ref, m_ref, g_ref, w_ref, uc_ref, ul_ref, qkv_ref, gate_ref,
                 col_ref, tmp_ref, *, n_ctx, tiles_per_seq, heads):
    i = pl.program_id(0)
    tm = xc_ref.shape[0]
    is_ctx = i < n_ctx

    @pl.when(i == 0)
    def _():
        lane = lax.broadcasted_iota(jnp.int32, (tm, 128), 1)
        tok = lax.broadcasted_iota(jnp.int32, (tm, 128), 0)
        inv = jnp.exp((lane & 31).astype(F32) * (-math.log(ROPE_BASE) / 32))
        ang = (tok % GRID_W).astype(F32) * inv
        col_ref[0] = jnp.cos(ang)
        col_ref[1] = jnp.sin(ang)

    m = m_ref[0]
    x = jnp.where(is_ctx, xc_ref[...], xl_ref[...])
    h = _rms_mod(x, g_ref[...], m[0:1], m[1:2]).astype(BF16)
    uw = uc_ref.shape[1]
    u = _dot(h, w_ref[:, 0:uw]).astype(uc_ref.dtype)

    @pl.when(is_ctx)
    def _():
        uc_ref[...] = u

    @pl.when(jnp.logical_not(is_ctx))
    def _():
        ul_ref[...] = u

    rw = gate_ref.shape[1]
    dh = rw // heads
    tile_in_seq = jnp.maximum(i - n_ctx, 0) % tiles_per_seq
    cos_t, sin_t = _rope_tables(tile_in_seq, col_ref[0], col_ref[1], jnp.logical_not(is_ctx))
    def rotate(part):
        for hh in range(heads):
            xh = tmp_ref[part, :, hh * dh:(hh + 1) * dh]
            y = xh * cos_t + pltpu.roll(xh, dh // 2, 1) * sin_t
            if part == 1:
                y = y * dh ** -0.5
            qkv_ref[:, part * rw + hh * dh:part * rw + (hh + 1) * dh] = y.astype(qkv_ref.dtype)

    tmp_ref[0] = _dot(h, w_ref[:, uw:uw + rw])
    tmp_ref[1] = _dot(h, w_ref[:, uw + rw:uw + 2 * rw])
    rotate(0)
    qkv_ref[:, 2 * rw:3 * rw] = _dot(h, w_ref[:, uw + 2 * rw:uw + 3 * rw]).astype(qkv_ref.dtype)
    rotate(1)
    gate_ref[...] = _dot(h, w_ref[:, uw + 3 * rw:uw + 4 * rw])


def _proj(lay, ctx2d, x2d, mods, g, w, uw, heads):
    d = x2d.shape[1]
    rw = (w.shape[1] - uw) // 4
    tm = ROW_TILE
    n_ctx = lay.ctx_rows // tm
    assert rw // heads == 128 and tm % GRID_W == 0
    return pl.pallas_call(
        functools.partial(_proj_kernel, n_ctx=n_ctx, tiles_per_seq=lay.seq // tm, heads=heads),
        out_shape=(jax.ShapeDtypeStruct((lay.ctx_rows, uw), BF16),
                   jax.ShapeDtypeStruct((x2d.shape[0], uw), BF16),
                   jax.ShapeDtypeStruct((lay.rows, 3 * rw), BF16),
                   jax.ShapeDtypeStruct((lay.rows, rw), F32)),
        grid=(lay.rows // tm,),
        in_specs=_two_source(lay, tm, d) + [
            pl.BlockSpec((1,) + mods.shape[1:], lambda i: (lay.mod_row(i, tm), 0, 0)),
            _const_spec((1, d)),
            _const_spec(w.shape)],
        out_specs=tuple(_two_source(lay, tm, uw) + [pl.BlockSpec((tm, 3 * rw), lambda i: (i, 0)),
                                                    pl.BlockSpec((tm, rw), lambda i: (i, 0))]),
        scratch_shapes=[pltpu.VMEM((2, tm, 128), F32), pltpu.VMEM((2, tm, rw), F32)],
        compiler_params=_cparams(("arbitrary",)),
        name="even_in_proj",
    )(ctx2d, x2d, mods, g.reshape(1, d), w)


def _mlp_kernel(x_ref, m_ref, g_ref, w1_ref, w2_ref, o_ref, *, hc):
    x = x_ref[...]
    m = m_ref[0]
    h = _rms_mod(x, g_ref[...], m[3:4], m[4:5]).astype(BF16)
    acc = jnp.zeros(x.shape, F32)
    for c in range(w1_ref.shape[1] // hc):
        a = _dot(h, w1_ref[:, c * hc:(c + 1) * hc])
        a = jnp.square(jnp.maximum(a, 0.0)).astype(BF16)
        acc = acc + _dot(a, w2_ref[c * hc:(c + 1) * hc, :])
    o_ref[...] = x + m[5:6] * acc


def _layer_spec(w, layer):
    nd = w.ndim - 1
    return pl.BlockSpec((None,) + w.shape[1:], lambda *_: (layer,) + (0,) * nd,
                        pipeline_mode=pl.Buffered(1))


def _mlp(lay, stream, mods, g, w1, w2, layer):
    rows, d = stream.shape
    tm = ROW_TILE
    return pl.pallas_call(
        functools.partial(_mlp_kernel, hc=512),
        out_shape=jax.ShapeDtypeStruct((rows, d), F32),
        grid=(rows // tm,),
        in_specs=[pl.BlockSpec((tm, d), lambda i: (i, 0)),
                  pl.BlockSpec((1,) + mods.shape[1:], lambda i: (lay.mod_row(i, tm), 0, 0)),
                  _const_spec((1, d)),
                  _layer_spec(w1, layer),
                  _layer_spec(w2, layer)],
        out_specs=pl.BlockSpec((tm, d), lambda i: (i, 0)),
        compiler_params=_cparams(("arbitrary",)),
        name="mlp",
    )(stream, mods, g.reshape(1, d), w1, w2)


def _s5_disc_kernel(lre_ref, lim_ref, ls_ref, bre_ref, bim_ref,
                    lam_ref, bbar_ref, pow_ref, *, seg_ctx, seg_lat):
    lre, lim = lre_ref[...], lim_ref[...]
    step = jnp.exp(ls_ref[...])
    mag = jnp.exp(lre * step)
    ang = lim * step
    lbr, lbi = mag * jnp.cos(ang), mag * jnp.sin(ang)
    lam_ref[0], lam_ref[1] = lbr, lbi
    den = lre * lre + lim * lim
    nr, ni = lbr - 1.0, lbi
    cr = (nr * lre + ni * lim) / den
    ci = (ni * lre - nr * lim) / den
    bre, bim = bre_ref[...], bim_ref[...]
    bbar_ref[0] = cr[:, None] * bre - ci[:, None] * bim
    bbar_ref[1] = cr[:, None] * bim + ci[:, None] * bre
    for k, n in enumerate((seg_ctx, seg_lat)):
        mag_n = jnp.exp(lre * step * n)
        ang_n = lim * step * n
        pow_ref[2 * k] = mag_n * jnp.cos(ang_n)
        pow_ref[2 * k + 1] = mag_n * jnp.sin(ang_n)


def _s5_discretize(lam_re, lam_im, log_step, b_re, b_im, seg_ctx, seg_lat):
    _, g, p = lam_re.shape
    n = b_re.shape[-1]
    gp = g * p
    lt = gp // 128
    shp = (2, lt, 128)
    ls = jnp.broadcast_to(log_step[:, :, None], (2, g, p)).reshape(shp)
    bre = b_re.reshape(2, gp, n).transpose(0, 2, 1).reshape(2, n, lt, 128)
    bim = b_im.reshape(2, gp, n).transpose(0, 2, 1).reshape(2, n, lt, 128)
    lam, bbar, pw = pl.pallas_call(
        functools.partial(_s5_disc_kernel, seg_ctx=seg_ctx, seg_lat=seg_lat),
        out_shape=(jax.ShapeDtypeStruct((2,) + shp, F32),
                   jax.ShapeDtypeStruct((2, 2, n, lt, 128), F32),
                   jax.ShapeDtypeStruct((4,) + shp, F32)),
        name="s5_discretize",
    )(lam_re.reshape(shp), lam_im.reshape(shp), ls, bre, bim)
    return lam.reshape(2, 2, gp), bbar.reshape(2, 2, n, gp), pw.reshape(4, 2, gp)


def _s5_scan_block(u_refs, wb_ref, wc_ref, lam_ref, st_ref, h_ref, y_refs, *, tb, final_pass):
    nkt, nrow, gw = st_ref.shape[1], st_ref.shape[3], st_ref.shape[4]
    lane_tiles = [slice(c * 128, (c + 1) * 128) for c in range(gw // 128)]

    pieces = [(2 * s + d, d, s) for s in range(nkt) for d in range(2)]

    def input_matmul(d, kt):
        u = u_refs[d][:, kt * 128:(kt + 1) * 128]
        return _dot(u, wb_ref[d, kt, 0]), _dot(u, wb_ref[d, kt, 1])

    def scan(hb, d, kt, bu_re, bu_im):
        lr = [jnp.broadcast_to(lam_ref[d, kt, 0, :, cs], (nrow, 128)) for cs in lane_tiles]
        li = [jnp.broadcast_to(lam_ref[d, kt, 1, :, cs], (nrow, 128)) for cs in lane_tiles]
        hr = [st_ref[d, kt, 0, :, cs] for cs in lane_tiles]
        hi = [st_ref[d, kt, 1, :, cs] for cs in lane_tiles]
        for t in range(tb):
            rows = slice(t * nrow, (t + 1) * nrow)
            br, bi = bu_re[rows, :], bu_im[rows, :]
            for c, cs in enumerate(lane_tiles):
                xr = lr[c] * hr[c] - li[c] * hi[c] + br[:, cs]
                xi = lr[c] * hi[c] + li[c] * hr[c] + bi[:, cs]
                if final_pass:
                    h_ref[hb, 0, rows, cs] = xr
                    h_ref[hb, 1, rows, cs] = xi
                hr[c], hi[c] = xr, xi
        for c, cs in enumerate(lane_tiles):
            st_ref[d, kt, 0, :, cs] = hr[c]
            st_ref[d, kt, 1, :, cs] = hi[c]

    ahead = [input_matmul(d, kt) for _, d, kt in pieces[:S5_MATMUL_AHEAD]]
    for i, (hb, d, kt) in enumerate(pieces):
        if i + S5_MATMUL_AHEAD < len(pieces):
            _, d2, kt2 = pieces[i + S5_MATMUL_AHEAD]
            ahead.append(input_matmul(d2, kt2))
        scan(hb, d, kt, *ahead.pop(0))
        if final_pass:
            y_refs[d, :, kt * 128:(kt + 1) * 128] = (
                _dot(h_ref[hb, 0].astype(BF16), wc_ref[d, kt, 0])
                - _dot(h_ref[hb, 1].astype(BF16), wc_ref[d, kt, 1]))


def _s5_scan_kernel(*refs, tb, ncb, nblk, final_pass):
    if final_pass:
        (utf_ref, utb_ref, permt_ref, wb_ref, wc_ref, lam_ref,
         pow_ref, fin_ref, ycf_ref, ylf_ref, ycb_ref, ylb_ref,
         st_ref, init_ref, h_ref, y_scr) = refs
    else:
        (ucf_ref, ulf_ref, ucb_ref, ulb_ref, perm_ref, wb_ref, lam_ref, fin_ref, utf_ref, utb_ref,
         st_ref) = refs
        wc_ref, h_ref, y_scr = None, None, None
    u_tm = (utf_ref, utb_ref)
    i = pl.program_id(0)
    part = jnp.where(i < ncb, 0, 1)
    first = jnp.logical_or(i == 0, i == ncb)
    last = jnp.logical_or(i == ncb - 1, i == nblk - 1)
    nseg = S5_SEGS
    nkt, nrow = st_ref.shape[1], st_ref.shape[3]

    @pl.when(i == 0)
    def _():
        if final_pass:
            for d in range(2):
                for kt in range(nkt):
                    for b in range(nrow // nseg):
                        cr = jnp.zeros((1, st_ref.shape[4]), F32)
                        ci = jnp.zeros((1, st_ref.shape[4]), F32)
                        for prt in range(2):
                            pr = pow_ref[d, prt, kt, 0]
                            pi = pow_ref[d, prt, kt, 1]
                            for s in range(nseg):
                                seg = s if d == 0 else nseg - 1 - s
                                r = b * nseg + seg
                                init_ref[d, prt, kt, 0, r:r + 1, :] = cr
                                init_ref[d, prt, kt, 1, r:r + 1, :] = ci
                                fr = fin_ref[d, prt, kt, 0, r:r + 1, :]
                                fi = fin_ref[d, prt, kt, 1, r:r + 1, :]
                                cr, ci = pr * cr - pi * ci + fr, pr * ci + pi * cr + fi

    @pl.when(first)
    def _():
        if final_pass:
            for d in range(2):
                st_ref[d] = jnp.where(part == 0, init_ref[d, 0], init_ref[d, 1])
        else:
            st_ref[...] = jnp.zeros(st_ref.shape, F32)

    if not final_pass:
        blk = tb * nrow
        for d, (uc_ref, ul_ref) in enumerate(((ucf_ref, ulf_ref), (ucb_ref, ulb_ref))):
            u_nat = jnp.where(part == 0, uc_ref[...], ul_ref[...]).reshape(blk, uc_ref.shape[2])
            u_tm[d][...] = _dot(perm_ref[d], u_nat).astype(BF16)

    _s5_scan_block(u_tm, wb_ref, wc_ref, lam_ref, st_ref, h_ref, y_scr,
                   tb=tb, final_pass=final_pass)

    if final_pass:
        for d, (yc_ref, yl_ref) in enumerate(((ycf_ref, ylf_ref), (ycb_ref, ylb_ref))):
            y_nat = _dot(permt_ref[d], y_scr[d].astype(BF16)).astype(BF16).reshape(yc_ref.shape)

            @pl.when(part == 0)
            def _(yc_ref=yc_ref, y_nat=y_nat):
                yc_ref[...] = y_nat

            @pl.when(part == 1)
            def _(yl_ref=yl_ref, y_nat=y_nat):
                yl_ref[...] = y_nat

    if not final_pass:
        @pl.when(last)
        def _():
            for d in range(2):
                for prt in range(2):
                    @pl.when(part == prt)
                    def _(d=d, prt=prt):
                        fin_ref[d, prt] = st_ref[d]


def _s5_scan(u_c, u_l, u_tm, perm, wb, wc, lam, pw, fin, *, final_pass):
    nkt, nrow, gw = fin.shape[2], fin.shape[4], fin.shape[5]
    uw = u_c.shape[2]
    tb = S5_TB
    ncb, nlb = u_c.shape[1] // tb, u_l.shape[1] // tb
    nblk = ncb + nlb
    ordered = pl.BlockSpec((tb * nrow, uw), lambda i: (i, 0))

    def c_f(i):
        return jnp.minimum(i, ncb - 1)

    def l_f(i):
        return jnp.clip(i - ncb, 0, nlb - 1)

    def spec(pos):
        return pl.BlockSpec((nrow, tb, uw), lambda i: (0, pos(i), 0))

    tiles = [spec(c_f), spec(l_f), spec(lambda i: ncb - 1 - c_f(i)), spec(lambda i: nlb - 1 - l_f(i))]
    kern = functools.partial(_s5_scan_kernel, tb=tb, ncb=ncb, nblk=nblk, final_pass=final_pass)
    state = pltpu.VMEM((2, nkt, 2, nrow, gw), F32)
    if final_pass:
        perm_t = perm.transpose(0, 2, 1)
        return pl.pallas_call(
            kern,
            out_shape=(jax.ShapeDtypeStruct(u_c.shape, BF16), jax.ShapeDtypeStruct(u_l.shape, BF16)) * 2,
            grid=(nblk,),
            in_specs=[ordered, ordered, _const_spec(perm.shape),
                      _const_spec(wb.shape), _const_spec(wc.shape),
                      _const_spec(lam.shape), _const_spec(pw.shape), _const_spec(fin.shape)],
            out_specs=tuple(tiles),
            scratch_shapes=[state, pltpu.VMEM(fin.shape, F32),
                            pltpu.VMEM((2 * nkt, 2, tb * nrow, gw), F32),
                            pltpu.VMEM((2, tb * nrow, uw), F32)],
            compiler_params=_cparams(("arbitrary",)),
            name="s5_scan_out",
        )(*u_tm, perm_t, wb, wc, lam, pw, fin)
    ordered_u = jax.ShapeDtypeStruct((nblk * tb * nrow, uw), BF16)
    return pl.pallas_call(
        kern,
        out_shape=(jax.ShapeDtypeStruct(fin.shape, F32), ordered_u, ordered_u),
        grid=(nblk,),
        in_specs=tiles + [_const_spec(perm.shape), _const_spec(wb.shape), _const_spec(lam.shape)],
        out_specs=(pl.BlockSpec(fin.shape, lambda i: (0,) * len(fin.shape)), ordered, ordered),
        scratch_shapes=[state],
        compiler_params=_cparams(("arbitrary",)),
        name="s5_scan_state",
    )(u_c, u_l, u_c, u_l, perm, wb, lam)


def _s5_mixer(lay, u_ctx, u_lat, lam_re, lam_im, log_step, b_re, b_im, c_re, c_im):
    bsz = lay.batch
    _, g, pst = lam_re.shape
    ngrp = b_re.shape[-1]
    uw = g * ngrp
    half = g * pst
    nseg = S5_SEGS
    seg_ctx, seg_lat = lay.ctx_len // nseg, lay.seq // nseg
    nrow = bsz * nseg

    lam, bbar, pw = _s5_discretize(lam_re, lam_im, log_step, b_re, b_im, seg_ctx, seg_lat)

    kt_g = 128 // ngrp
    kt_n = g // kt_g
    bb = bbar.reshape(2, 2, ngrp, kt_n, kt_g, pst)
    eye = jnp.eye(kt_g, dtype=bool)
    wb = jnp.where(eye[None, None, None, :, None, :, None],
                   bb.transpose(0, 1, 3, 2, 4, 5)[:, :, :, None, :, :, :], 0.0)
    wb = wb.reshape(2, 2, kt_n, kt_g * ngrp, kt_g * pst)
    wb = jnp.stack([wb[0], wb[1]], axis=2).astype(BF16)

    gw = kt_g * pst

    def c_tiles(c):
        cc = c.reshape(2, kt_n, kt_g, ngrp, pst).transpose(0, 1, 2, 4, 3)
        w = jnp.where(eye[None, None, :, None, :, None], cc[:, :, :, :, None, :], 0.0)
        return w.reshape(2, kt_n, gw, kt_g * ngrp)

    wc = jnp.stack([c_tiles(c_re), c_tiles(c_im)], axis=2).astype(BF16)

    def grouped(a, lead):
        a = a.reshape(lead + (2, 2, kt_n, 1, gw))
        perm = (len(lead) + 1,) + tuple(range(len(lead))) + (len(lead) + 2, len(lead), len(lead) + 3,
                                                             len(lead) + 4)
        return a.transpose(perm)

    lam_g = grouped(lam, ())
    pw_g = grouped(pw.reshape(2, 2, 2, half), (2,))

    tb = S5_TB
    out_row = jnp.arange(tb * nrow)[:, None]
    in_row = jnp.arange(tb * nrow)[None, :]
    t_o, r_o = out_row // nrow, out_row % nrow
    perm = jnp.stack([(in_row == r_o * tb + tt).astype(BF16) for tt in (t_o, tb - 1 - t_o)])

    u_c = u_ctx.reshape(nrow, seg_ctx, uw)
    u_l = u_lat.reshape(nrow, seg_lat, uw)
    fin0 = jnp.zeros((2, 2, kt_n, 2, nrow, gw), F32)
    fin, ut_f, ut_b = _s5_scan(u_c, u_l, None, perm, wb, None, lam_g, None, fin0, final_pass=False)
    ycf, ylf, ycb, ylb = _s5_scan(u_c, u_l, (ut_f, ut_b), perm, wb, wc, lam_g, pw_g, fin,
                                  final_pass=True)
    flat = lambda y: y.reshape(-1, uw)
    return flat(ycf), flat(ylf), flat(ycb), flat(ylb)


def _rope_tables(chunk_idx, col_cos, col_sin, is_lat):
    chunk = col_cos.shape[0]
    lane = lax.broadcasted_iota(jnp.int32, (8, 128), 1)
    sub = lax.broadcasted_iota(jnp.int32, (8, 128), 0)
    quarter = 32
    inv = jnp.exp((lane & (quarter - 1)).astype(F32) * (-math.log(ROPE_BASE) / quarter))
    rows_per_chunk = chunk // GRID_W
    ang = (chunk_idx * rows_per_chunk + sub).astype(F32) * inv
    rc, rs = jnp.cos(ang), jnp.sin(ang)
    tok = lax.broadcasted_iota(jnp.int32, (chunk, 128), 0)
    lane_t = lax.broadcasted_iota(jnp.int32, (chunk, 128), 1)
    row_cos = jnp.zeros((chunk, 128), F32)
    row_sin = jnp.zeros((chunk, 128), F32)
    for r in range(rows_per_chunk):
        sel = (tok // GRID_W) == r
        row_cos = jnp.where(sel, rc[r:r + 1, :], row_cos)
        row_sin = jnp.where(sel, rs[r:r + 1, :], row_sin)
    is_col = (lane_t & quarter) != 0
    cos_t = jnp.where(is_col, col_cos, row_cos)
    sin_t = jnp.where(is_col, col_sin, row_sin)
    sin_t = jnp.where(lane_t < 64, -sin_t, sin_t)
    cos_t = jnp.where(is_lat, cos_t, 1.0)
    sin_t = jnp.where(is_lat, sin_t, 0.0)
    return cos_t, sin_t


def _ret_kernel(dl_ref, qf_ref, kf_ref, vf_ref, qb_ref, kb_ref, vb_ref, of_ref, ob_ref,
                st_ref, intra_ref, tab_ref, *, heads):
    i = pl.program_id(1)
    dh = qf_ref.shape[1] // heads
    chunk = qf_ref.shape[0]

    @pl.when(i == 0)
    def _():
        st_ref[...] = jnp.zeros(st_ref.shape, F32)
        ii = lax.broadcasted_iota(jnp.int32, (chunk, chunk), 0)
        jj = lax.broadcasted_iota(jnp.int32, (chunk, chunk), 1)
        relf = (ii - jj).astype(F32)
        rowf = lax.broadcasted_iota(jnp.int32, (chunk, dh), 0).astype(F32)
        for d in range(2):
            for h in range(heads):
                lg = _log_sigmoid(jnp.full((chunk, chunk), dl_ref[d, h], F32))
                lgv = _log_sigmoid(jnp.full((chunk, dh), dl_ref[d, h], F32))
                if d == 0:
                    mask = ii >= jj
                    intra = jnp.where(mask, jnp.exp(jnp.where(mask, relf, 0.0) * lg), 0.0)
                    dq = jnp.exp((rowf + 1.0) * lgv)
                    dk = jnp.exp((chunk - 1.0 - rowf) * lgv)
                else:
                    mask = jj > ii
                    intra = jnp.where(mask, jnp.exp(jnp.where(mask, -relf, 0.0) * lg), 0.0)
                    dq = jnp.exp((chunk - rowf) * lgv)
                    dk = jnp.exp(rowf * lgv)
                intra_ref[d, h] = intra
                tab_ref[d, h, 0] = dq
                tab_ref[d, h, 1] = dk
                tab_ref[d, h, 2] = jnp.exp(chunk * lgv)

    for d, (q_ref, k_ref, v_ref, o_ref) in enumerate(((qf_ref, kf_ref, vf_ref, of_ref),
                                                      (qb_ref, kb_ref, vb_ref, ob_ref))):
        for h in range(heads):
            q = q_ref[:, h * dh:(h + 1) * dh]
            k = k_ref[:, h * dh:(h + 1) * dh]
            v = v_ref[:, h * dh:(h + 1) * dh]
            state = st_ref[d, h]
            s = _dot_nt(q, k) * intra_ref[d, h]
            o = _dot(s.astype(BF16), v) + _dot(q, state.astype(BF16)) * tab_ref[d, h, 0]
            o_ref[:, h * dh:(h + 1) * dh] = o.astype(o_ref.dtype)
            kdt = (k.astype(F32) * tab_ref[d, h, 1]).T.astype(BF16)
            st_ref[d, h] = state * tab_ref[d, h, 2, 0:dh, :] + _dot(kdt, v)


def _retention(lay, qkv, decay_logit):
    heads = decay_logit.shape[1]
    width = qkv.shape[1] // 3
    dh = width // heads
    chunk = RET_CHUNK
    assert chunk >= dh and lay.ctx_len % chunk == 0 and lay.seq % chunk == 0
    ncc, nlc = lay.ctx_len // chunk, lay.seq // chunk
    orders = [functools.partial(f, chunk=chunk) for f in (lay.chunk_fwd, lay.chunk_bwd)]

    def spec(col, idx):
        return pl.BlockSpec((chunk, width), lambda b, i: (idx(b, i), col))

    ins = [spec(c, f) for f in orders for c in range(3)]
    outs = tuple(pl.BlockSpec((chunk, width), lambda b, i, f=f: (f(b, i), 0)) for f in orders)
    return pl.pallas_call(
        functools.partial(_ret_kernel, heads=heads),
        out_shape=(jax.ShapeDtypeStruct((lay.rows, width), BF16),) * 2,
        grid=(lay.batch, ncc + nlc),
        in_specs=[pl.BlockSpec(memory_space=pltpu.SMEM)] + ins,
        out_specs=outs,
        scratch_shapes=[pltpu.VMEM((2, heads, dh, dh), F32),
                        pltpu.VMEM((2, heads, chunk, chunk), F32),
                        pltpu.VMEM((2, heads, 3, chunk, dh), F32)],
        compiler_params=_cparams(("arbitrary", "arbitrary")),
        name="retention",
    )(decay_logit, qkv, qkv, qkv, qkv, qkv, qkv)


def _even_out_kernel(xc_ref, xl_ref, uc_ref, ul_ref, yfc_ref, yfl_ref, ybc_ref, ybl_ref, m_ref,
                     rf_ref, rb_ref, g_ref, d_ref, wglu_ref, bglu_ref, gn_ref, wout_ref, o_ref,
                     *, n_ctx, head_dim):
    is_ctx = pl.program_id(0) < n_ctx

    def pick(c_ref, l_ref):
        return jnp.where(is_ctx, c_ref[...], l_ref[...])

    m = m_ref[0]
    uw = uc_ref.shape[1]
    x = pick(xc_ref, xl_ref)
    s5 = (pick(uc_ref, ul_ref).astype(F32) * d_ref[...]
          + pick(yfc_ref, yfl_ref).astype(F32) + pick(ybc_ref, ybl_ref).astype(F32))
    ab = _dot(jax.nn.gelu(s5).astype(BF16), wglu_ref[...]) + bglu_ref[...]
    s5_o = ab[:, :uw] * _sigmoid(ab[:, uw:])
    ret = _group_norm(rf_ref[...].astype(F32) + rb_ref[...].astype(F32), head_dim) * gn_ref[...]
    ret_o = ret * _silu(g_ref[...])
    y = _dot(s5_o.astype(BF16), wout_ref[0:uw, :]) + _dot(ret_o.astype(BF16), wout_ref[uw:, :])
    o_ref[...] = x + m[2:3] * y


def _even_out(lay, ctx2d, x2d, mods, u_parts, y_parts, gate, rf, rb, s5_d, w_glu, b_glu, gn_g,
              w_out, heads):
    d = x2d.shape[1]
    uw = u_parts[0].shape[1]
    rw = rf.shape[1]
    tm = ROW_TILE

    def row(w, col=0):
        return pl.BlockSpec((tm, w), lambda i: (i, col))

    return pl.pallas_call(
        functools.partial(_even_out_kernel, n_ctx=lay.ctx_rows // tm, head_dim=rw // heads),
        out_shape=jax.ShapeDtypeStruct((lay.rows, d), F32),
        grid=(lay.rows // tm,),
        in_specs=(_two_source(lay, tm, d) + _two_source(lay, tm, uw) * 3
                  + [pl.BlockSpec((1,) + mods.shape[1:], lambda i: (lay.mod_row(i, tm), 0, 0)),
                     row(rw), row(rw), row(rw),
                     _const_spec((1, uw)), _const_spec(w_glu.shape), _const_spec((1, 2 * uw)),
                     _const_spec((1, rw)), _const_spec(w_out.shape)]),
        out_specs=row(d),
        compiler_params=_cparams(("arbitrary",)),
        name="even_out",
    )(ctx2d, x2d, *u_parts, *y_parts, mods, rf, rb, gate, s5_d.reshape(1, uw), w_glu,
      b_glu.reshape(1, 2 * uw), gn_g.reshape(1, rw), w_out)


def _ml_proj_kernel(x_ref, xp_ref, xn_ref, m_ref, g_ref, wx_ref, wo_ref, wgt_ref,
                    gbt_ref, cw_ref, cb_ref, wq_ref, wk_ref, wv_ref,
                    q_ref, k_ref, v_ref, op_ref, xc_ref, gc_ref, gr_ref, xm_scr,
                    *, heads, first_last, k_scale):
    i = pl.program_id(0)
    tm = x_ref.shape[0]
    halo = xp_ref.shape[0]
    m = m_ref[0]
    first, last = first_last(i)
    g = g_ref[...]
    hf = _rms_mod(x_ref[...], g, m[0:1], m[1:2])
    hp = _rms_mod(xp_ref[...], g, m[0:1], m[1:2]) * jnp.where(first, 0.0, 1.0)
    hn = _rms_mod(xn_ref[...], g, m[0:1], m[1:2]) * jnp.where(last, 0.0, 1.0)
    h = hf.astype(BF16)
    h_ext = jnp.concatenate([hp, hf, hn], axis=0).astype(BF16)

    def gate_stage():
        gates = _dot_nt(wgt_ref[...], h) + gbt_ref[...]
        si = lax.broadcasted_iota(jnp.int32, (tm, tm), 0)
        ti = lax.broadcasted_iota(jnp.int32, (tm, tm), 1)
        same = (si // CHUNK) == (ti // CHUNK)
        upto = jnp.where(jnp.logical_and(same, si <= ti), 1.0, 0.0)
        from_ = jnp.where(jnp.logical_and(same, si >= ti), 1.0, 0.0)
        kind = lax.broadcasted_iota(jnp.int32, gates.shape, 0) // heads
        ls = _log_sigmoid(gates)
        cum_f = _dot_f32(ls, upto)
        cum_b = _dot_f32(ls, from_)
        gr = jnp.where(kind == 1, cum_f, jnp.where(kind == 3, cum_b, gates))
        gr_ref[...] = gr
        pad = jnp.zeros((gc_ref.shape[1] - gr.shape[0], tm), F32)
        gc_ref[...] = jnp.concatenate([gr, pad], axis=0).T

    width = wx_ref.shape[1]
    cn = xm_scr.shape[2]
    taps = cw_ref.shape[0]

    def project(c):
        cs = slice(c * cn, (c + 1) * cn)
        op_ref[:, cs] = _dot(h, wo_ref[:, cs]).astype(op_ref.dtype)
        xm_scr[c] = _dot(h_ext, wx_ref[:, cs])

    gate_stage()
    for c in range(ML_PROJ_AHEAD):
        project(c)
    for c in range(width // cn):
        cs = slice(c * cn, (c + 1) * cn)
        if c + ML_PROJ_AHEAD < width // cn:
            project(c + ML_PROJ_AHEAD)
        xm = xm_scr[c, halo:halo + tm, :]
        acc = jnp.zeros((tm, cn), F32) + cb_ref[:, cs]
        for t in range(taps):
            off = halo + t - taps // 2
            acc = acc + xm_scr[c, off:off + tm, :] * cw_ref[t:t + 1, cs]
        xc = _silu(acc)
        xc_ref[:, cs] = xc.astype(xc_ref.dtype)
        xcb, xmb = xc.astype(BF16), xm.astype(BF16)
        bw = wq_ref.shape[1]
        for j in range(cn // bw):
            blk = c * (cn // bw) + j
            ls_ = slice(j * bw, (j + 1) * bw)
            os_ = slice(c * cn + j * bw, c * cn + (j + 1) * bw)
            q_ref[:, os_] = _dot(xcb[:, ls_], wq_ref[blk]).astype(BF16)
            k_ref[os_, :] = (_dot_nt(wk_ref[blk], xcb[:, ls_]) * k_scale).astype(BF16)
            v_ref[:, os_] = _dot(xmb[:, ls_], wv_ref[blk]).astype(BF16)


def _blockdiag_tiles(w, tile):
    nb, blk, _ = w.shape
    per = tile // blk
    rows = jnp.tile(w.reshape(nb // per, tile, blk), (1, 1, per))
    a_of_row = jnp.arange(tile)[:, None] // blk
    b_of_lane = jnp.arange(tile)[None, :] // blk
    return jnp.where(a_of_row == b_of_lane, rows, 0.0).astype(BF16)


def _ml_proj(lay, stream, mods, g, w_in, gate_b, conv_w, conv_b, wq, wk, wv, heads):
    rows, d = stream.shape
    inner = conv_w.shape[1]
    ng = 4 * heads
    tm = ML_ROW_TILE
    halo = CONV_HALO
    assert lay.ctx_len % tm == 0 and lay.seq % tm == 0 and tm % CHUNK == 0
    n_ctx = lay.ctx_rows // tm
    ctx_per, lat_per = lay.ctx_len // tm, lay.seq // tm

    def first_last(i):
        j = jnp.where(i < n_ctx, i % ctx_per, (i - n_ctx) % lat_per)
        per = jnp.where(i < n_ctx, ctx_per, lat_per)
        return j == 0, j == per - 1

    w_all = w_in.astype(BF16)
    w_gt = w_in[:, 2 * inner:].T.astype(BF16)
    gb_r = gate_b.reshape(ng, 1)
    tile = 256
    wq_t, wk_t, wv_t = (_blockdiag_tiles(w, tile) for w in (wq, wk, wv))
    wk_t = wk_t.transpose(0, 2, 1)
    r8 = tm // halo
    last8 = rows // halo - 1

    tok = lambda w, dt: jax.ShapeDtypeStruct((rows, w), dt)
    return pl.pallas_call(
        functools.partial(_ml_proj_kernel, heads=heads, first_last=first_last,
                          k_scale=(inner // heads) ** -0.5),
        out_shape=(tok(inner, BF16), jax.ShapeDtypeStruct((inner, rows), BF16), tok(inner, BF16),
                   tok(inner, BF16), tok(inner, BF16), tok(128, F32),
                   jax.ShapeDtypeStruct((ng, rows), F32)),
        grid=(rows // tm,),
        in_specs=[pl.BlockSpec((tm, d), lambda i: (i, 0)),
                  pl.BlockSpec((halo, d), lambda i: (jnp.maximum(i * r8 - 1, 0), 0)),
                  pl.BlockSpec((halo, d), lambda i: (jnp.minimum((i + 1) * r8, last8), 0)),
                  pl.BlockSpec((1,) + mods.shape[1:], lambda i: (lay.mod_row(i, tm), 0, 0)),
                  _const_spec((1, d)),
                  pl.BlockSpec((d, inner), lambda i: (0, 0), pipeline_mode=pl.Buffered(1)),
                  pl.BlockSpec((d, inner), lambda i: (0, 1), pipeline_mode=pl.Buffered(1)),
                  _const_spec(w_gt.shape),
                  _const_spec(gb_r.shape), _const_spec(conv_w.shape), _const_spec((1, inner)),
                  _const_spec(wq_t.shape), _const_spec(wk_t.shape), _const_spec(wv_t.shape)],
        out_specs=tuple([pl.BlockSpec((tm, inner), lambda i: (i, 0)),
                         pl.BlockSpec((inner, tm), lambda i: (0, i))]
                        + [pl.BlockSpec((tm, inner), lambda i: (i, 0))] * 3
                        + [pl.BlockSpec((tm, 128), lambda i: (i, 0)),
                           pl.BlockSpec((ng, tm), lambda i: (0, i))]),
        scratch_shapes=[pltpu.VMEM((inner // ML_COL_CHUNK, tm + 2 * halo, ML_COL_CHUNK), F32)],
        compiler_params=_cparams(("arbitrary",)),
        name="mlstm_in_proj",
    )(stream, stream, stream, mods, g.reshape(1, d), w_all, w_all, w_gt, gb_r,
      conv_w, conv_b.reshape(1, inner), wq_t, wk_t, wv_t)


def _ml_scan_kernel(qf_ref, kf_ref, vf_ref, gcf_ref, grf_ref, qb_ref, kb_ref, vb_ref, gcb_ref,
                    grb_ref, hf_ref, hb_ref, c_ref, cb_ref, m_ref, *, heads):
    i = pl.program_id(1)
    dh = qf_ref.shape[1] // heads
    cols = c_ref.shape[3]
    cw = 256

    @pl.when(i == 0)
    def _():
        c_ref[...] = jnp.zeros(c_ref.shape, F32)
        cb_ref[...] = jnp.zeros(cb_ref.shape, BF16)
        m_ref[...] = jnp.zeros(m_ref.shape, F32)

    ii = lax.broadcasted_iota(jnp.int32, (CHUNK, CHUNK), 0)
    jj = lax.broadcasted_iota(jnp.int32, (CHUNK, CHUNK), 1)
    lane0 = lax.broadcasted_iota(jnp.int32, (CHUNK, cols - dh), 1) == 0
    one_col = jnp.where(lane0, 1.0, 0.0).astype(BF16)
    tok_r = lax.broadcasted_iota(jnp.int32, (1, CHUNK), 1)
    dir_refs = ((qf_ref, kf_ref, vf_ref, gcf_ref, grf_ref, hf_ref),
                (qb_ref, kb_ref, vb_ref, gcb_ref, grb_ref, hb_ref))
    for h in range(heads):
        for d, (q_ref, k_ref, v_ref, gc_ref, gr_ref, h_ref) in enumerate(dir_refs):
            mask = (jj <= ii) if d == 0 else (jj >= ii)
            end = CHUNK - 1 if d == 0 else 0
            hs = slice(h * dh, (h + 1) * dh)
            ki, kb = (2 * d) * heads + h, (2 * d + 1) * heads + h
            q, kt, v = q_ref[:, hs], k_ref[hs, :], v_ref[:, hs]
            b_c = gc_ref[:, kb:kb + 1]
            ig_r = gr_ref[ki:ki + 1, :]
            b_r = gr_ref[kb:kb + 1, :]
            m_prev = m_ref[d, h, 0:1, 0:1]
            log_w = jnp.where(mask, b_c - b_r + ig_r, -jnp.inf)
            log_prev = b_c + m_prev
            m_row = jnp.maximum(log_prev, jnp.max(log_w, axis=-1, keepdims=True))
            w = jnp.exp(log_w - m_row)
            w_prev = jnp.exp(log_prev - m_row)
            s = _dot(q, kt) * w
            qn = _dot(q, cb_ref[d, h, :, dh:cols])
            qn = jnp.sum(jnp.where(lane0, qn, 0.0), axis=-1, keepdims=True)
            num = _dot(s.astype(BF16), v) + w_prev * _dot(q, cb_ref[d, h, :, 0:dh])
            den = jnp.sum(s, axis=-1, keepdims=True) + w_prev * qn
            h_ref[:, hs] = (num / jnp.maximum(jnp.abs(den), jnp.exp(-m_row))).astype(h_ref.dtype)
            b_last = jnp.sum(jnp.where(tok_r == end, b_r, 0.0), axis=-1, keepdims=True)
            log_k = b_last - b_r + ig_r
            m_new = jnp.maximum(b_last + m_prev, jnp.max(log_k, axis=-1, keepdims=True))
            w_k = jnp.exp(log_k - m_new)
            w_c = jnp.exp(b_last + m_prev - m_new)
            kwt = (kt.astype(F32) * w_k).astype(BF16)
            for c0 in range(0, cols, cw):
                c1 = min(c0 + cw, cols)
                rhs = v_ref[:, h * dh + c0:h * dh + c1] if c1 <= dh else one_col
                cnew = w_c * c_ref[d, h, :, c0:c1] + _dot(kwt, rhs)
                c_ref[d, h, :, c0:c1] = cnew
                cb_ref[d, h, :, c0:c1] = cnew.astype(BF16)
            m_ref[d, h] = jnp.broadcast_to(m_new, m_ref.shape[2:])


def _ml_scan(lay, q, k, v, gc, gr, heads):
    inner = q.shape[1]
    dh = inner // heads
    ncc, nlc = lay.ctx_len // CHUNK, lay.seq // CHUNK
    ng = gr.shape[0]
    ins = []
    for f in (lay.chunk_fwd, lay.chunk_bwd):
        tok = pl.BlockSpec((CHUNK, inner), lambda b, i, f=f: (f(b, i), 0))
        ins += [tok, pl.BlockSpec((inner, CHUNK), lambda b, i, f=f: (0, f(b, i))), tok]
        ins += [pl.BlockSpec((CHUNK, 128), lambda b, i, f=f: (f(b, i), 0)),
                pl.BlockSpec((ng, CHUNK), lambda b, i, f=f: (0, f(b, i)))]
    outs = tuple(pl.BlockSpec((CHUNK, inner), lambda b, i, f=f: (f(b, i), 0))
                 for f in (lay.chunk_fwd, lay.chunk_bwd))
    return pl.pallas_call(
        functools.partial(_ml_scan_kernel, heads=heads),
        out_shape=(jax.ShapeDtypeStruct((lay.rows, inner), BF16),) * 2,
        grid=(lay.batch, ncc + nlc),
        in_specs=ins,
        out_specs=outs,
        scratch_shapes=[pltpu.VMEM((2, heads, dh, dh + 128), F32),
                        pltpu.VMEM((2, heads, dh, dh + 128), BF16),
                        pltpu.VMEM((2, heads, 8, 128), F32)],
        compiler_params=_cparams(("arbitrary", "arbitrary")),
        name="mlstm_scan",
    )(q, k, v, gc, gr, q, k, v, gc, gr)


def _ml_tail_kernel(x_ref, m_ref, hf_ref, hb_ref, xc_ref, op_ref, gn_ref, sk_ref, wout_ref,
                    g_ref, w1_ref, w2_ref, fg_ref, o_ref, act_ref, *, head_dim, hc):
    i = pl.program_id(0)
    fill, drain = i % 2, 1 - i % 2
    tm = x_ref.shape[0]
    n_chunks = w1_ref.shape[1] // hc
    rc = tm // n_chunks

    @pl.when(i == 0)
    def _():
        act_ref[...] = jnp.zeros(act_ref.shape, act_ref.dtype)

    def prepare(c):
        rs = slice(c * rc, (c + 1) * rc)
        hsum = hf_ref[rs, :].astype(F32) + hb_ref[rs, :].astype(F32)
        hn = _group_norm(hsum, head_dim) * gn_ref[...] + sk_ref[...] * xc_ref[rs, :].astype(F32)
        act_ref[fill, rs, :] = (_sigmoid(op_ref[rs, :].astype(F32)) * hn).astype(act_ref.dtype)

    m = m_ref[0]
    lat = x_ref[...] + m[2:3] * _dot(act_ref[drain], wout_ref[...])
    h = _rms_mod(lat, g_ref[...], m[3:4], m[4:5]).astype(BF16)
    acc = jnp.zeros(lat.shape, F32)
    for c in range(n_chunks):
        a = _dot(h, w1_ref[:, c * hc:(c + 1) * hc])
        a = jnp.square(jnp.maximum(a, 0.0)).astype(BF16)
        acc = acc + _dot(a, w2_ref[c * hc:(c + 1) * hc, :])
        prepare(c)
    y = lat + m[5:6] * acc
    o_ref[...] = y * lax.rsqrt(jnp.mean(y * y, axis=-1, keepdims=True) + EPS) * fg_ref[...]


def _ml_tail(lay, stream, mods, hf, hb, xc, o_pre, gn_g, skip, w_out, g, w1, w2, layer, final_g,
             heads):
    d = stream.shape[1]
    inner = hf.shape[1]
    tm = ROW_TILE
    sk = lay.ctx_rows // tm
    n_tiles = lay.rows // tm - sk
    per_b = lay.seq // tm

    def cur(w):
        return pl.BlockSpec((tm, w), lambda i: (jnp.minimum(i, n_tiles - 1) + sk, 0))

    def prev(i):
        return jnp.maximum(i - 1, 0)

    return pl.pallas_call(
        functools.partial(_ml_tail_kernel, head_dim=inner // heads, hc=512),
        out_shape=jax.ShapeDtypeStruct((n_tiles * tm, d), F32),
        grid=(n_tiles + 1,),
        in_specs=[pl.BlockSpec((tm, d), lambda i: (prev(i) + sk, 0)),
                  pl.BlockSpec((1,) + mods.shape[1:], lambda i: (prev(i) // per_b, 0, 0)),
                  cur(inner), cur(inner), cur(inner), cur(inner),
                  _const_spec((1, inner)), _const_spec((1, inner)), _const_spec(w_out.shape),
                  _const_spec((1, d)), _layer_spec(w1, layer), _layer_spec(w2, layer),
                  _const_spec((1, d))],
        out_specs=pl.BlockSpec((tm, d), lambda i: (prev(i), 0)),
        scratch_shapes=[pltpu.VMEM((2, tm, inner), BF16)],
        compiler_params=_cparams(("arbitrary",)),
        name="mlstm_out_mlp_final",
    )(stream, mods, hf, hb, xc, o_pre, gn_g.reshape(1, inner), skip.reshape(1, inner), w_out,
      g.reshape(1, d), w1, w2, final_g.reshape(1, d))


def kernel(x, c, ctx, c_ctx, mod_w, mod_b, norm_mix_g, norm_mlp_g, mlp_w1, mlp_w2, final_norm_g,
           ev_w_in, ev_w_out, s5_lambda_re, s5_lambda_im, s5_log_step, s5_b_re, s5_b_im, s5_c_re,
           s5_c_im, s5_d, s5_w_glu, s5_b_glu, ret_decay_logit, ret_gn_g,
           ml_w_in, ml_gate_b, ml_conv_w, ml_conv_b, ml_wq, ml_wk, ml_wv, ml_gn_g, ml_skip,
           ml_w_out):
    bsz, seq, d = x.shape
    ctx_len = ctx.shape[1]
    depth = mod_w.shape[0]
    assert depth == 2, "one even (S5 + retention) layer followed by one mLSTM layer"
    lay = _Layout(bsz, ctx_len, seq)
    assert lay.ctx_rows % ROW_TILE == 0 and seq % ROW_TILE == 0
    assert ctx_len % (S5_SEGS * S5_TB) == 0 and seq % (S5_SEGS * S5_TB) == 0
    assert ctx_len % CHUNK == 0 and seq % CHUNK == 0

    c_all = jnp.concatenate([c, c_ctx[None, :], jnp.zeros((8 - bsz - 1, d), F32)], axis=0)
    mods = _mod_table(c_all, mod_w, mod_b)
    ctx2d, x2d = ctx.reshape(bsz * ctx_len, d), x.reshape(bsz * seq, d)

    ret_w = ret_gn_g.shape[1]
    ret_heads = ret_decay_logit.shape[2]
    u_ctx, u_lat, qkv, gate = _proj(lay, ctx2d, x2d, mods[0], norm_mix_g[0],
                                    ev_w_in[0].astype(BF16), s5_d.shape[1], ret_heads)
    y_parts = _s5_mixer(lay, u_ctx, u_lat, s5_lambda_re[0], s5_lambda_im[0], s5_log_step[0],
                        s5_b_re[0], s5_b_im[0], s5_c_re[0], s5_c_im[0])
    rf, rb = _retention(lay, qkv, ret_decay_logit[0])
    stream = _even_out(lay, ctx2d, x2d, mods[0], (u_ctx, u_lat), y_parts, gate, rf, rb, s5_d[0],
                       s5_w_glu[0].astype(BF16), s5_b_glu[0], ret_gn_g[0],
                       ev_w_out[0].astype(BF16), ret_heads)
    w1_all, w2_all = mlp_w1.astype(BF16), mlp_w2.astype(BF16)
    stream = _mlp(lay, stream, mods[0], norm_mlp_g[0], w1_all, w2_all, 0)

    ml_heads = ml_gate_b.shape[1] // 4
    q, k, v, o_pre, xc, gc, gr = _ml_proj(lay, stream, mods[1], norm_mix_g[1], ml_w_in[0],
                                          ml_gate_b[0], ml_conv_w[0], ml_conv_b[0],
                                          ml_wq[0], ml_wk[0], ml_wv[0], ml_heads)
    hf, hb = _ml_scan(lay, q, k, v, gc, gr, ml_heads)
    out = _ml_tail(lay, stream, mods[1], hf, hb, xc, o_pre, ml_gn_g[0], ml_skip[0],
                   ml_w_out[0].astype(BF16), norm_mlp_g[1], w1_all, w2_all, 1, final_norm_g,
                   ml_heads)
    return out.reshape(bsz, seq, d)
```

```python
import functools
import math

import jax
import jax.numpy as jnp
from jax import lax
from jax.experimental import pallas as pl
from jax.experimental.pallas import tpu as pltpu

F32 = jnp.float32
BF16 = jnp.bfloat16

EPS = 1e-6
GRID_W = 64
ROPE_BASE = 10000.0
CHUNK = 256
RET_CHUNK = 256
ML_QKV_BLOCK = 4
V7X_SCOPED_VMEM_BYTES = 60000 * 1024
S5_SEGS = 4
S5_TB = 32
S5_MATMUL_AHEAD = 1
ROW_TILE = 512
ML_ROW_TILE = 256
CONV_HALO = 8
ML_COL_CHUNK = 512
ML_PROJ_AHEAD = 1


def _cparams(sem, vmem_bytes=V7X_SCOPED_VMEM_BYTES):
    return pltpu.CompilerParams(dimension_semantics=sem, vmem_limit_bytes=vmem_bytes)


def _const_spec(shape):
    nd = len(shape)
    return pl.BlockSpec(shape, lambda *_: (0,) * nd, pipeline_mode=pl.Buffered(1))


def _dot(a, b):
    return jnp.dot(a, b, preferred_element_type=F32)


def _dot_nt(a, b):
    return lax.dot_general(a, b, (((1,), (1,)), ((), ())), preferred_element_type=F32)


def _dot_tn(a, b):
    return lax.dot_general(a, b, (((0,), (0,)), ((), ())), preferred_element_type=F32)


def _dot_f32(a, b):
    return jnp.dot(a, b, preferred_element_type=F32, precision=lax.Precision.HIGHEST)


def _sigmoid(x):
    return jax.nn.sigmoid(x)


def _silu(x):
    return x * jax.nn.sigmoid(x)


def _log_sigmoid(x):
    return jnp.minimum(x, 0.0) - jnp.log1p(jnp.exp(-jnp.abs(x)))


def _rms_mod(x, g, shift, scale):
    y = x * lax.rsqrt(jnp.mean(x * x, axis=-1, keepdims=True) + EPS) * g
    return y * (1.0 + scale) + shift


def _group_norm(x, width):
    outs = []
    for h in range(x.shape[-1] // width):
        xh = x[:, h * width:(h + 1) * width]
        xc = xh - jnp.mean(xh, axis=-1, keepdims=True)
        var = jnp.mean(xc * xc, axis=-1, keepdims=True)
        outs.append(xc * lax.rsqrt(var + EPS))
    return jnp.concatenate(outs, axis=-1)


class _Layout:
    def __init__(self, batch, ctx_len, seq):
        self.batch, self.ctx_len, self.seq = batch, ctx_len, seq
        self.ctx_rows = batch * ctx_len
        self.rows = batch * (ctx_len + seq)

    def mod_row(self, tile, tile_rows):
        n_ctx = self.ctx_rows // tile_rows
        return jnp.where(tile < n_ctx, self.batch, (tile - n_ctx) // (self.seq // tile_rows))

    def chunk_fwd(self, b, i, chunk=CHUNK):
        ncc, nlc = self.ctx_len // chunk, self.seq // chunk
        return jnp.where(i < ncc, b * ncc + i, self.batch * ncc + b * nlc + (i - ncc))

    def chunk_bwd(self, b, i, chunk=CHUNK):
        ncc, nlc = self.ctx_len // chunk, self.seq // chunk
        return jnp.where(i < ncc, b * ncc + (ncc - 1 - i),
                         self.batch * ncc + b * nlc + (nlc - 1 - (i - ncc)))


def _mod_kernel(c_ref, w_ref, b_ref, o_ref):
    s = _silu(c_ref[...]).astype(BF16)
    o_ref[0] = _dot(s, w_ref[0].astype(BF16)) + b_ref[0]


def _mod_table(c_all, mod_w, mod_b):
    depth, d, n = mod_w.shape
    tn = 1536
    out = pl.pallas_call(
        _mod_kernel,
        out_shape=jax.ShapeDtypeStruct((depth, c_all.shape[0], n), F32),
        grid=(depth, n // tn),
        in_specs=[pl.BlockSpec(c_all.shape, lambda l, j: (0, 0)),
                  pl.BlockSpec((1, d, tn), lambda l, j: (l, 0, j)),
                  pl.BlockSpec((1, 1, tn), lambda l, j: (l, 0, j))],
        out_specs=pl.BlockSpec((1, c_all.shape[0], tn), lambda l, j: (l, 0, j)),
        compiler_params=_cparams(("arbitrary", "arbitrary")),
        name="mod_table",
    )(c_all, mod_w, mod_b.reshape(depth, 1, n))
    return out.reshape(depth, c_all.shape[0], n // d, d)


def _two_source(lay, tm, width):
    n_ctx = lay.ctx_rows // tm
    return [pl.BlockSpec((tm, width), lambda i: (jnp.minimum(i, n_ctx - 1), 0)),
            pl.BlockSpec((tm, width), lambda i: (jnp.maximum(i - n_ctx, 0), 0))]


def _proj_kernel(xc_ref, xl_ref, m_ref, g_ref, w_ref, uc_ref, ul_ref, qkv_ref, gate_ref,
                 col_ref, tmp_ref, *, n_ctx, tiles_per_seq, heads):
    i = pl.program_id(0)
    tm = xc_ref.shape[0]
    is_ctx = i < n_ctx

    @pl.when(i == 0)
    def _():
        lane = lax.broadcasted_iota(jnp.int32, (tm, 128), 1)
        tok = lax.broadcasted_iota(jnp.int32, (tm, 128), 0)
        inv = jnp.exp((lane & 31).astype(F32) * (-math.log(ROPE_BASE) / 32))
        ang = (tok % GRID_W).astype(F32) * inv
        col_ref[0] = jnp.cos(ang)
        col_ref[1] = jnp.sin(ang)

    m = m_ref[0]
    x = jnp.where(is_ctx, xc_ref[...], xl_ref[...])
    h = _rms_mod(x, g_ref[...], m[0:1], m[1:2]).astype(BF16)
    uw = uc_ref.shape[1]
    u = _dot(h, w_ref[:, 0:uw]).astype(uc_ref.dtype)

    @pl.when(is_ctx)
    def _():
        uc_ref[...] = u

    @pl.when(jnp.logical_not(is_ctx))
    def _():
        ul_ref[...] = u

    rw = gate_ref.shape[1]
    dh = rw // heads
    tile_in_seq = jnp.maximum(i - n_ctx, 0) % tiles_per_seq
    cos_t, sin_t = _rope_tables(tile_in_seq, col_ref[0], col_ref[1], jnp.logical_not(is_ctx))
    def rotate(part):
        for hh in range(heads):
            xh = tmp_ref[part, :, hh * dh:(hh + 1) * dh]
            y = xh * cos_t + pltpu.roll(xh, dh // 2, 1) * sin_t
            if part == 1:
                y = y * dh ** -0.5
            qkv_ref[:, part * rw + hh * dh:part * rw + (hh + 1) * dh] = y.astype(qkv_ref.dtype)

    tmp_ref[0] = _dot(h, w_ref[:, uw:uw + rw])
    tmp_ref[1] = _dot(h, w_ref[:, uw + rw:uw + 2 * rw])
    rotate(0)
    qkv_ref[:, 2 * rw:3 * rw] = _dot(h, w_ref[:, uw + 2 * rw:uw + 3 * rw]).astype(qkv_ref.dtype)
    rotate(1)
    gate_ref[...] = _dot(h, w_ref[:, uw + 3 * rw:uw + 4 * rw])


def _proj(lay, ctx2d, x2d, mods, g, w, uw, heads):
    d = x2d.shape[1]
    rw = (w.shape[1] - uw) // 4
    tm = ROW_TILE
    n_ctx = lay.ctx_rows // tm
    assert rw // heads == 128 and tm % GRID_W == 0
    return pl.pallas_call(
        functools.partial(_proj_kernel, n_ctx=n_ctx, tiles_per_seq=lay.seq // tm, heads=heads),
        out_shape=(jax.ShapeDtypeStruct((lay.ctx_rows, uw), BF16),
                   jax.ShapeDtypeStruct((x2d.shape[0], uw), BF16),
                   jax.ShapeDtypeStruct((lay.rows, 3 * rw), BF16),
                   jax.ShapeDtypeStruct((lay.rows, rw), F32)),
        grid=(lay.rows // tm,),
        in_specs=_two_source(lay, tm, d) + [
            pl.BlockSpec((1,) + mods.shape[1:], lambda i: (lay.mod_row(i, tm), 0, 0)),
            _const_spec((1, d)),
            _const_spec(w.shape)],
        out_specs=tuple(_two_source(lay, tm, uw) + [pl.BlockSpec((tm, 3 * rw), lambda i: (i, 0)),
                                                    pl.BlockSpec((tm, rw), lambda i: (i, 0))]),
        scratch_shapes=[pltpu.VMEM((2, tm, 128), F32), pltpu.VMEM((2, tm, rw), F32)],
        compiler_params=_cparams(("arbitrary",)),
        name="even_in_proj",
    )(ctx2d, x2d, mods, g.reshape(1, d), w)


def _mlp_kernel(x_ref, m_ref, g_ref, w1_ref, w2_ref, o_ref, *, hc):
    x = x_ref[...]
    m = m_ref[0]
    h = _rms_mod(x, g_ref[...], m[3:4], m[4:5]).astype(BF16)
    acc = jnp.zeros(x.shape, F32)
    for c in range(w1_ref.shape[1] // hc):
        a = _dot(h, w1_ref[:, c * hc:(c + 1) * hc])
        a = jnp.square(jnp.maximum(a, 0.0)).astype(BF16)
        acc = acc + _dot(a, w2_ref[c * hc:(c + 1) * hc, :])
    o_ref[...] = x + m[5:6] * acc


def _layer_spec(w, layer):
    nd = w.ndim - 1
    return pl.BlockSpec((None,) + w.shape[1:], lambda *_: (layer,) + (0,) * nd,
                        pipeline_mode=pl.Buffered(1))


def _mlp(lay, stream, mods, g, w1, w2, layer):
    rows, d = stream.shape
    tm = ROW_TILE
    return pl.pallas_call(
        functools.partial(_mlp_kernel, hc=512),
        out_shape=jax.ShapeDtypeStruct((rows, d), F32),
        grid=(rows // tm,),
        in_specs=[pl.BlockSpec((tm, d), lambda i: (i, 0)),
                  pl.BlockSpec((1,) + mods.shape[1:], lambda i: (lay.mod_row(i, tm), 0, 0)),
                  _const_spec((1, d)),
                  _layer_spec(w1, layer),
                  _layer_spec(w2, layer)],
        out_specs=pl.BlockSpec((tm, d), lambda i: (i, 0)),
        compiler_params=_cparams(("arbitrary",)),
        name="mlp",
    )(stream, mods, g.reshape(1, d), w1, w2)


def _s5_disc_kernel(lre_ref, lim_ref, ls_ref, bre_ref, bim_ref,
                    lam_ref, bbar_ref, pow_ref, *, seg_ctx, seg_lat):
    lre, lim = lre_ref[...], lim_ref[...]
    step = jnp.exp(ls_ref[...])
    mag = jnp.exp(lre * step)
    ang = lim * step
    lbr, lbi = mag * jnp.cos(ang), mag * jnp.sin(ang)
    lam_ref[0], lam_ref[1] = lbr, lbi
    den = lre * lre + lim * lim
    nr, ni = lbr - 1.0, lbi
    cr = (nr * lre + ni * lim) / den
    ci = (ni * lre - nr * lim) / den
    bre, bim = bre_ref[...], bim_ref[...]
    bbar_ref[0] = cr[:, None] * bre - ci[:, None] * bim
    bbar_ref[1] = cr[:, None] * bim + ci[:, None] * bre
    for k, n in enumerate((seg_ctx, seg_lat)):
        mag_n = jnp.exp(lre * step * n)
        ang_n = lim * step * n
        pow_ref[2 * k] = mag_n * jnp.cos(ang_n)
        pow_ref[2 * k + 1] = mag_n * jnp.sin(ang_n)


def _s5_discretize(lam_re, lam_im, log_step, b_re, b_im, seg_ctx, seg_lat):
    _, g, p = lam_re.shape
    n = b_re.shape[-1]
    gp = g * p
    lt = gp // 128
    shp = (2, lt, 128)
    ls = jnp.broadcast_to(log_step[:, :, None], (2, g, p)).reshape(shp)
    bre = b_re.reshape(2, gp, n).transpose(0, 2, 1).reshape(2, n, lt, 128)
    bim = b_im.reshape(2, gp, n).transpose(0, 2, 1).reshape(2, n, lt, 128)
    lam, bbar, pw = pl.pallas_call(
        functools.partial(_s5_disc_kernel, seg_ctx=seg_ctx, seg_lat=seg_lat),
        out_shape=(jax.ShapeDtypeStruct((2,) + shp, F32),
                   jax.ShapeDtypeStruct((2, 2, n, lt, 128), F32),
                   jax.ShapeDtypeStruct((4,) + shp, F32)),
        name="s5_discretize",
    )(lam_re.reshape(shp), lam_im.reshape(shp), ls, bre, bim)
    return lam.reshape(2, 2, gp), bbar.reshape(2, 2, n, gp), pw.reshape(4, 2, gp)


def _s5_scan_block(u_refs, wb_ref, wc_ref, lam_ref, st_ref, h_ref, y_refs, *, tb, final_pass):
    nkt, nrow, gw = st_ref.shape[1], st_ref.shape[3], st_ref.shape[4]
    lane_tiles = [slice(c * 128, (c + 1) * 128) for c in range(gw // 128)]

    pieces = [(2 * s + d, d, s) for s in range(nkt) for d in range(2)]

    def input_matmul(d, kt):
        u = u_refs[d][:, kt * 128:(kt + 1) * 128]
        return _dot(u, wb_ref[d, kt, 0]), _dot(u, wb_ref[d, kt, 1])

    def scan(hb, d, kt, bu_re, bu_im):
        lr = [jnp.broadcast_to(lam_ref[d, kt, 0, :, cs], (nrow, 128)) for cs in lane_tiles]
        li = [jnp.broadcast_to(lam_ref[d, kt, 1, :, cs], (nrow, 128)) for cs in lane_tiles]
        hr = [st_ref[d, kt, 0, :, cs] for cs in lane_tiles]
        hi = [st_ref[d, kt, 1, :, cs] for cs in lane_tiles]
        for t in range(tb):
            rows = slice(t * nrow, (t + 1) * nrow)
            br, bi = bu_re[rows, :], bu_im[rows, :]
            for c, cs in enumerate(lane_tiles):
                xr = lr[c] * hr[c] - li[c] * hi[c] + br[:, cs]
                xi = lr[c] * hi[c] + li[c] * hr[c] + bi[:, cs]
                if final_pass:
                    h_ref[hb, 0, rows, cs] = xr
                    h_ref[hb, 1, rows, cs] = xi
                hr[c], hi[c] = xr, xi
        for c, cs in enumerate(lane_tiles):
            st_ref[d, kt, 0, :, cs] = hr[c]
            st_ref[d, kt, 1, :, cs] = hi[c]

    ahead = [input_matmul(d, kt) for _, d, kt in pieces[:S5_MATMUL_AHEAD]]
    for i, (hb, d, kt) in enumerate(pieces):
        if i + S5_MATMUL_AHEAD < len(pieces):
            _, d2, kt2 = pieces[i + S5_MATMUL_AHEAD]
            ahead.append(input_matmul(d2, kt2))
        scan(hb, d, kt, *ahead.pop(0))
        if final_pass:
            y_refs[d, :, kt * 128:(kt + 1) * 128] = (
                _dot(h_ref[hb, 0].astype(BF16), wc_ref[d, kt, 0])
                - _dot(h_ref[hb, 1].astype(BF16), wc_ref[d, kt, 1]))


def _s5_scan_kernel(*refs, tb, ncb, nblk, final_pass):
    if final_pass:
        (utf_ref, utb_ref, permt_ref, wb_ref, wc_ref, lam_ref,
         pow_ref, fin_ref, ycf_ref, ylf_ref, ycb_ref, ylb_ref,
         st_ref, init_ref, h_ref, y_scr) = refs
    else:
        (ucf_ref, ulf_ref, ucb_ref, ulb_ref, perm_ref, wb_ref, lam_ref, fin_ref, utf_ref, utb_ref,
         st_ref) = refs
        wc_ref, h_ref, y_scr = None, None, None
    u_tm = (utf_ref, utb_ref)
    i = pl.program_id(0)
    part = jnp.where(i < ncb, 0, 1)
    first = jnp.logical_or(i == 0, i == ncb)
    last = jnp.logical_or(i == ncb - 1, i == nblk - 1)
    nseg = S5_SEGS
    nkt, nrow = st_ref.shape[1], st_ref.shape[3]

    @pl.when(i == 0)
    def _():
        if final_pass:
            for d in range(2):
                for kt in range(nkt):
                    for b in range(nrow // nseg):
                        cr = jnp.zeros((1, st_ref.shape[4]), F32)
                        ci = jnp.zeros((1, st_ref.shape[4]), F32)
                        for prt in range(2):
                            pr = pow_ref[d, prt, kt, 0]
                            pi = pow_ref[d, prt, kt, 1]
                            for s in range(nseg):
                                seg = s if d == 0 else nseg - 1 - s
                                r = b * nseg + seg
                                init_ref[d, prt, kt, 0, r:r + 1, :] = cr
                                init_ref[d, prt, kt, 1, r:r + 1, :] = ci
                                fr = fin_ref[d, prt, kt, 0, r:r + 1, :]
                                fi = fin_ref[d, prt, kt, 1, r:r + 1, :]
                                cr, ci = pr * cr - pi * ci + fr, pr * ci + pi * cr + fi

    @pl.when(first)
    def _():
        if final_pass:
            for d in range(2):
                st_ref[d] = jnp.where(part == 0, init_ref[d, 0], init_ref[d, 1])
        else:
            st_ref[...] = jnp.zeros(st_ref.shape, F32)

    if not final_pass:
        blk = tb * nrow
        for d, (uc_ref, ul_ref) in enumerate(((ucf_ref, ulf_ref), (ucb_ref, ulb_ref))):
            u_nat = jnp.where(part == 0, uc_ref[...], ul_ref[...]).reshape(blk, uc_ref.shape[2])
            u_tm[d][...] = _dot(perm_ref[d], u_nat).astype(BF16)

    _s5_scan_block(u_tm, wb_ref, wc_ref, lam_ref, st_ref, h_ref, y_scr,
                   tb=tb, final_pass=final_pass)

    if final_pass:
        for d, (yc_ref, yl_ref) in enumerate(((ycf_ref, ylf_ref), (ycb_ref, ylb_ref))):
            y_nat = _dot(permt_ref[d], y_scr[d].astype(BF16)).astype(BF16).reshape(yc_ref.shape)

            @pl.when(part == 0)
            def _(yc_ref=yc_ref, y_nat=y_nat):
                yc_ref[...] = y_nat

            @pl.when(part == 1)
            def _(yl_ref=yl_ref, y_nat=y_nat):
                yl_ref[...] = y_nat

    if not final_pass:
        @pl.when(last)
        def _():
            for d in range(2):
                for prt in range(2):
                    @pl.when(part == prt)
                    def _(d=d, prt=prt):
                        fin_ref[d, prt] = st_ref[d]


def _s5_scan(u_c, u_l, u_tm, perm, wb, wc, lam, pw, fin, *, final_pass):
    nkt, nrow, gw = fin.shape[2], fin.shape[4], fin.shape[5]
    uw = u_c.shape[2]
    tb = S5_TB
    ncb, nlb = u_c.shape[1] // tb, u_l.shape[1] // tb
    nblk = ncb + nlb
    ordered = pl.BlockSpec((tb * nrow, uw), lambda i: (i, 0))

    def c_f(i):
        return jnp.minimum(i, ncb - 1)

    def l_f(i):
        return jnp.clip(i - ncb, 0, nlb - 1)

    def spec(pos):
        return pl.BlockSpec((nrow, tb, uw), lambda i: (0, pos(i), 0))

    tiles = [spec(c_f), spec(l_f), spec(lambda i: ncb - 1 - c_f(i)), spec(lambda i: nlb - 1 - l_f(i))]
    kern = functools.partial(_s5_scan_kernel, tb=tb, ncb=ncb, nblk=nblk, final_pass=final_pass)
    state = pltpu.VMEM((2, nkt, 2, nrow, gw), F32)
    if final_pass:
        perm_t = perm.transpose(0, 2, 1)
        return pl.pallas_call(
            kern,
            out_shape=(jax.ShapeDtypeStruct(u_c.shape, BF16), jax.ShapeDtypeStruct(u_l.shape, BF16)) * 2,
            grid=(nblk,),
            in_specs=[ordered, ordered, _const_spec(perm.shape),
                      _const_spec(wb.shape), _const_spec(wc.shape),
                      _const_spec(lam.shape), _const_spec(pw.shape), _const_spec(fin.shape)],
            out_specs=tuple(tiles),
            scratch_shapes=[state, pltpu.VMEM(fin.shape, F32),
                            pltpu.VMEM((2 * nkt, 2, tb * nrow, gw), F32),
                            pltpu.VMEM((2, tb * nrow, uw), F32)],
            compiler_params=_cparams(("arbitrary",)),
            name="s5_scan_out",
        )(*u_tm, perm_t, wb, wc, lam, pw, fin)
    ordered_u = jax.ShapeDtypeStruct((nblk * tb * nrow, uw), BF16)
    return pl.pallas_call(
        kern,
        out_shape=(jax.ShapeDtypeStruct(fin.shape, F32), ordered_u, ordered_u),
        grid=(nblk,),
        in_specs=tiles + [_const_spec(perm.shape), _const_spec(wb.shape), _const_spec(lam.shape)],
        out_specs=(pl.BlockSpec(fin.shape, lambda i: (0,) * len(fin.shape)), ordered, ordered),
        scratch_shapes=[state],
        compiler_params=_cparams(("arbitrary",)),
        name="s5_scan_state",
    )(u_c, u_l, u_c, u_l, perm, wb, lam)


def _s5_mixer(lay, u_ctx, u_lat, lam_re, lam_im, log_step, b_re, b_im, c_re, c_im):
    bsz = lay.batch
    _, g, pst = lam_re.shape
    ngrp = b_re.shape[-1]
    uw = g * ngrp
    half = g * pst
    nseg = S5_SEGS
    seg_ctx, seg_lat = lay.ctx_len // nseg, lay.seq // nseg
    nrow = bsz * nseg

    lam, bbar, pw = _s5_discretize(lam_re, lam_im, log_step, b_re, b_im, seg_ctx, seg_lat)

    kt_g = 128 // ngrp
    kt_n = g // kt_g
    bb = bbar.reshape(2, 2, ngrp, kt_n, kt_g, pst)
    eye = jnp.eye(kt_g, dtype=bool)
    wb = jnp.where(eye[None, None, None, :, None, :, None],
                   bb.transpose(0, 1, 3, 2, 4, 5)[:, :, :, None, :, :, :], 0.0)
    wb = wb.reshape(2, 2, kt_n, kt_g * ngrp, kt_g * pst)
    wb = jnp.stack([wb[0], wb[1]], axis=2).astype(BF16)

    gw = kt_g * pst

    def c_tiles(c):
        cc = c.reshape(2, kt_n, kt_g, ngrp, pst).transpose(0, 1, 2, 4, 3)
        w = jnp.where(eye[None, None, :, None, :, None], cc[:, :, :, :, None, :], 0.0)
        return w.reshape(2, kt_n, gw, kt_g * ngrp)

    wc = jnp.stack([c_tiles(c_re), c_tiles(c_im)], axis=2).astype(BF16)

    def grouped(a, lead):
        a = a.reshape(lead + (2, 2, kt_n, 1, gw))
        perm = (len(lead) + 1,) + tuple(range(len(lead))) + (len(lead) + 2, len(lead), len(lead) + 3,
                                                             len(lead) + 4)
        return a.transpose(perm)

    lam_g = grouped(lam, ())
    pw_g = grouped(pw.reshape(2, 2, 2, half), (2,))

    tb = S5_TB
    out_row = jnp.arange(tb * nrow)[:, None]
    in_row = jnp.arange(tb * nrow)[None, :]
    t_o, r_o = out_row // nrow, out_row % nrow
    perm = jnp.stack([(in_row == r_o * tb + tt).astype(BF16) for tt in (t_o, tb - 1 - t_o)])

    u_c = u_ctx.reshape(nrow, seg_ctx, uw)
    u_l = u_lat.reshape(nrow, seg_lat, uw)
    fin0 = jnp.zeros((2, 2, kt_n, 2, nrow, gw), F32)
    fin, ut_f, ut_b = _s5_scan(u_c, u_l, None, perm, wb, None, lam_g, None, fin0, final_pass=False)
    ycf, ylf, ycb, ylb = _s5_scan(u_c, u_l, (ut_f, ut_b), perm, wb, wc, lam_g, pw_g, fin,
                                  final_pass=True)
    flat = lambda y: y.reshape(-1, uw)
    return flat(ycf), flat(ylf), flat(ycb), flat(ylb)


def _rope_tables(chunk_idx, col_cos, col_sin, is_lat):
    chunk = col_cos.shape[0]
    lane = lax.broadcasted_iota(jnp.int32, (8, 128), 1)
    sub = lax.broadcasted_iota(jnp.int32, (8, 128), 0)
    quarter = 32
    inv = jnp.exp((lane & (quarter - 1)).astype(F32) * (-math.log(ROPE_BASE) / quarter))
    rows_per_chunk = chunk // GRID_W
    ang = (chunk_idx * rows_per_chunk + sub).astype(F32) * inv
    rc, rs = jnp.cos(ang), jnp.sin(ang)
    tok = lax.broadcasted_iota(jnp.int32, (chunk, 128), 0)
    lane_t = lax.broadcasted_iota(jnp.int32, (chunk, 128), 1)
    row_cos = jnp.zeros((chunk, 128), F32)
    row_sin = jnp.zeros((chunk, 128), F32)
    for r in range(rows_per_chunk):
        sel = (tok // GRID_W) == r
        row_cos = jnp.where(sel, rc[r:r + 1, :], row_cos)
        row_sin = jnp.where(sel, rs[r:r + 1, :], row_sin)
    is_col = (lane_t & quarter) != 0
    cos_t = jnp.where(is_col, col_cos, row_cos)
    sin_t = jnp.where(is_col, col_sin, row_sin)
    sin_t = jnp.where(lane_t < 64, -sin_t, sin_t)
    cos_t = jnp.where(is_lat, cos_t, 1.0)
    sin_t = jnp.where(is_lat, sin_t, 0.0)
    return cos_t, sin_t


def _ret_kernel(dl_ref, qf_ref, kf_ref, vf_ref, qb_ref, kb_ref, vb_ref, of_ref, ob_ref,
                st_ref, intra_ref, tab_ref, *, heads):
    i = pl.program_id(1)
    dh = qf_ref.shape[1] // heads
    chunk = qf_ref.shape[0]

    @pl.when(i == 0)
    def _():
        st_ref[...] = jnp.zeros(st_ref.shape, F32)
        ii = lax.broadcasted_iota(jnp.int32, (chunk, chunk), 0)
        jj = lax.broadcasted_iota(jnp.int32, (chunk, chunk), 1)
        relf = (ii - jj).astype(F32)
        rowf = lax.broadcasted_iota(jnp.int32, (chunk, dh), 0).astype(F32)
        for d in range(2):
            for h in range(heads):
                lg = _log_sigmoid(jnp.full((chunk, chunk), dl_ref[d, h], F32))
                lgv = _log_sigmoid(jnp.full((chunk, dh), dl_ref[d, h], F32))
                if d == 0:
                    mask = ii >= jj
                    intra = jnp.where(mask, jnp.exp(jnp.where(mask, relf, 0.0) * lg), 0.0)
                    dq = jnp.exp((rowf + 1.0) * lgv)
                    dk = jnp.exp((chunk - 1.0 - rowf) * lgv)
                else:
                    mask = jj > ii
                    intra = jnp.where(mask, jnp.exp(jnp.where(mask, -relf, 0.0) * lg), 0.0)
                    dq = jnp.exp((chunk - rowf) * lgv)
                    dk = jnp.exp(rowf * lgv)
                intra_ref[d, h] = intra
                tab_ref[d, h, 0] = dq
                tab_ref[d, h, 1] = dk
                tab_ref[d, h, 2] = jnp.exp(chunk * lgv)

    for d, (q_ref, k_ref, v_ref, o_ref) in enumerate(((qf_ref, kf_ref, vf_ref, of_ref),
                                                      (qb_ref, kb_ref, vb_ref, ob_ref))):
        for h in range(heads):
            q = q_ref[:, h * dh:(h + 1) * dh]
            k = k_ref[:, h * dh:(h + 1) * dh]
            v = v_ref[:, h * dh:(h + 1) * dh]
            state = st_ref[d, h]
            s = _dot_nt(q, k) * intra_ref[d, h]
            o = _dot(s.astype(BF16), v) + _dot(q, state.astype(BF16)) * tab_ref[d, h, 0]
            o_ref[:, h * dh:(h + 1) * dh] = o.astype(o_ref.dtype)
            kdt = (k.astype(F32) * tab_ref[d, h, 1]).T.astype(BF16)
            st_ref[d, h] = state * tab_ref[d, h, 2, 0:dh, :] + _dot(kdt, v)


def _retention(lay, qkv, decay_logit):
    heads = decay_logit.shape[1]
    width = qkv.shape[1] // 3
    dh = width // heads
    chunk = RET_CHUNK
    assert chunk >= dh and lay.ctx_len % chunk == 0 and lay.seq % chunk == 0
    ncc, nlc = lay.ctx_len // chunk, lay.seq // chunk
    orders = [functools.partial(f, chunk=chunk) for f in (lay.chunk_fwd, lay.chunk_bwd)]

    def spec(col, idx):
        return pl.BlockSpec((chunk, width), lambda b, i: (idx(b, i), col))

    ins = [spec(c, f) for f in orders for c in range(3)]
    outs = tuple(pl.BlockSpec((chunk, width), lambda b, i, f=f: (f(b, i), 0)) for f in orders)
    return pl.pallas_call(
        functools.partial(_ret_kernel, heads=heads),
        out_shape=(jax.ShapeDtypeStruct((lay.rows, width), BF16),) * 2,
        grid=(lay.batch, ncc + nlc),
        in_specs=[pl.BlockSpec(memory_space=pltpu.SMEM)] + ins,
        out_specs=outs,
        scratch_shapes=[pltpu.VMEM((2, heads, dh, dh), F32),
                        pltpu.VMEM((2, heads, chunk, chunk), F32),
                        pltpu.VMEM((2, heads, 3, chunk, dh), F32)],
        compiler_params=_cparams(("arbitrary", "arbitrary")),
        name="retention",
    )(decay_logit, qkv, qkv, qkv, qkv, qkv, qkv)


def _even_out_kernel(xc_ref, xl_ref, uc_ref, ul_ref, yfc_ref, yfl_ref, ybc_ref, ybl_ref, m_ref,
                     rf_ref, rb_ref, g_ref, d_ref, wglu_ref, bglu_ref, gn_ref, wout_ref, o_ref,
                     *, n_ctx, head_dim):
    is_ctx = pl.program_id(0) < n_ctx

    def pick(c_ref, l_ref):
        return jnp.where(is_ctx, c_ref[...], l_ref[...])

    m = m_ref[0]
    uw = uc_ref.shape[1]
    x = pick(xc_ref, xl_ref)
    s5 = (pick(uc_ref, ul_ref).astype(F32) * d_ref[...]
          + pick(yfc_ref, yfl_ref).astype(F32) + pick(ybc_ref, ybl_ref).astype(F32))
    ab = _dot(jax.nn.gelu(s5).astype(BF16), wglu_ref[...]) + bglu_ref[...]
    s5_o = ab[:, :uw] * _sigmoid(ab[:, uw:])
    ret = _group_norm(rf_ref[...].astype(F32) + rb_ref[...].astype(F32), head_dim) * gn_ref[...]
    ret_o = ret * _silu(g_ref[...])
    y = _dot(s5_o.astype(BF16), wout_ref[0:uw, :]) + _dot(ret_o.astype(BF16), wout_ref[uw:, :])
    o_ref[...] = x + m[2:3] * y


def _even_out(lay, ctx2d, x2d, mods, u_parts, y_parts, gate, rf, rb, s5_d, w_glu, b_glu, gn_g,
              w_out, heads):
    d = x2d.shape[1]
    uw = u_parts[0].shape[1]
    rw = rf.shape[1]
    tm = ROW_TILE

    def row(w, col=0):
        return pl.BlockSpec((tm, w), lambda i: (i, col))

    return pl.pallas_call(
        functools.partial(_even_out_kernel, n_ctx=lay.ctx_rows // tm, head_dim=rw // heads),
        out_shape=jax.ShapeDtypeStruct((lay.rows, d), F32),
        grid=(lay.rows // tm,),
        in_specs=(_two_source(lay, tm, d) + _two_source(lay, tm, uw) * 3
                  + [pl.BlockSpec((1,) + mods.shape[1:], lambda i: (lay.mod_row(i, tm), 0, 0)),
                     row(rw), row(rw), row(rw),
                     _const_spec((1, uw)), _const_spec(w_glu.shape), _const_spec((1, 2 * uw)),
                     _const_spec((1, rw)), _const_spec(w_out.shape)]),
        out_specs=row(d),
        compiler_params=_cparams(("arbitrary",)),
        name="even_out",
    )(ctx2d, x2d, *u_parts, *y_parts, mods, rf, rb, gate, s5_d.reshape(1, uw), w_glu,
      b_glu.reshape(1, 2 * uw), gn_g.reshape(1, rw), w_out)


def _ml_proj_kernel(x_ref, xp_ref, xn_ref, m_ref, g_ref, wx_ref, wo_ref, wgt_ref,
                    gbt_ref, cw_ref, cb_ref, wq_ref, wk_ref, wv_ref,
                    q_ref, k_ref, v_ref, op_ref, xc_ref, gc_ref, gr_ref, xm_scr,
                    *, heads, first_last, k_scale):
    i = pl.program_id(0)
    tm = x_ref.shape[0]
    halo = xp_ref.shape[0]
    m = m_ref[0]
    first, last = first_last(i)
    g = g_ref[...]
    hf = _rms_mod(x_ref[...], g, m[0:1], m[1:2])
    hp = _rms_mod(xp_ref[...], g, m[0:1], m[1:2]) * jnp.where(first, 0.0, 1.0)
    hn = _rms_mod(xn_ref[...], g, m[0:1], m[1:2]) * jnp.where(last, 0.0, 1.0)
    h = hf.astype(BF16)
    h_ext = jnp.concatenate([hp, hf, hn], axis=0).astype(BF16)

    def gate_stage():
        gates = _dot_nt(wgt_ref[...], h) + gbt_ref[...]
        si = lax.broadcasted_iota(jnp.int32, (tm, tm), 0)
        ti = lax.broadcasted_iota(jnp.int32, (tm, tm), 1)
        same = (si // CHUNK) == (ti // CHUNK)
        upto = jnp.where(jnp.logical_and(same, si <= ti), 1.0, 0.0)
        from_ = jnp.where(jnp.logical_and(same, si >= ti), 1.0, 0.0)
        kind = lax.broadcasted_iota(jnp.int32, gates.shape, 0) // heads
        ls = _log_sigmoid(gates)
        cum_f = _dot_f32(ls, upto)
        cum_b = _dot_f32(ls, from_)
        gr = jnp.where(kind == 1, cum_f, jnp.where(kind == 3, cum_b, gates))
        gr_ref[...] = gr
        pad = jnp.zeros((gc_ref.shape[1] - gr.shape[0], tm), F32)
        gc_ref[...] = jnp.concatenate([gr, pad], axis=0).T

    width = wx_ref.shape[1]
    cn = xm_scr.shape[2]
    taps = cw_ref.shape[0]

    def project(c):
        cs = slice(c * cn, (c + 1) * cn)
        op_ref[:, cs] = _dot(h, wo_ref[:, cs]).astype(op_ref.dtype)
        xm_scr[c] = _dot(h_ext, wx_ref[:, cs])

    gate_stage()
    for c in range(ML_PROJ_AHEAD):
        project(c)
    for c in range(width // cn):
        cs = slice(c * cn, (c + 1) * cn)
        if c + ML_PROJ_AHEAD < width // cn:
            project(c + ML_PROJ_AHEAD)
        xm = xm_scr[c, halo:halo + tm, :]
        acc = jnp.zeros((tm, cn), F32) + cb_ref[:, cs]
        for t in range(taps):
            off = halo + t - taps // 2
            acc = acc + xm_scr[c, off:off + tm, :] * cw_ref[t:t + 1, cs]
        xc = _silu(acc)
        xc_ref[:, cs] = xc.astype(xc_ref.dtype)
        xcb, xmb = xc.astype(BF16), xm.astype(BF16)
        bw = wq_ref.shape[1]
        for j in range(cn // bw):
            blk = c * (cn // bw) + j
            ls_ = slice(j * bw, (j + 1) * bw)
            os_ = slice(c * cn + j * bw, c * cn + (j + 1) * bw)
            q_ref[:, os_] = _dot(xcb[:, ls_], wq_ref[blk]).astype(BF16)
            k_ref[os_, :] = (_dot_nt(wk_ref[blk], xcb[:, ls_]) * k_scale).astype(BF16)
            v_ref[:, os_] = _dot(xmb[:, ls_], wv_ref[blk]).astype(BF16)


def _blockdiag_tiles(w, tile):
    nb, blk, _ = w.shape
    per = tile // blk
    rows = jnp.tile(w.reshape(nb // per, tile, blk), (1, 1, per))
    a_of_row = jnp.arange(tile)[:, None] // blk
    b_of_lane = jnp.arange(tile)[None, :] // blk
    return jnp.where(a_of_row == b_of_lane, rows, 0.0).astype(BF16)


def _ml_proj(lay, stream, mods, g, w_in, gate_b, conv_w, conv_b, wq, wk, wv, heads):
    rows, d = stream.shape
    inner = conv_w.shape[1]
    ng = 4 * heads
    tm = ML_ROW_TILE
    halo = CONV_HALO
    assert lay.ctx_len % tm == 0 and lay.seq % tm == 0 and tm % CHUNK == 0
    n_ctx = lay.ctx_rows // tm
    ctx_per, lat_per = lay.ctx_len // tm, lay.seq // tm

    def first_last(i):
        j = jnp.where(i < n_ctx, i % ctx_per, (i - n_ctx) % lat_per)
        per = jnp.where(i < n_ctx, ctx_per, lat_per)
        return j == 0, j == per - 1

    w_all = w_in.astype(BF16)
    w_gt = w_in[:, 2 * inner:].T.astype(BF16)
    gb_r = gate_b.reshape(ng, 1)
    tile = 256
    wq_t, wk_t, wv_t = (_blockdiag_tiles(w, tile) for w in (wq, wk, wv))
    wk_t = wk_t.transpose(0, 2, 1)
    r8 = tm // halo
    last8 = rows // halo - 1

    tok = lambda w, dt: jax.ShapeDtypeStruct((rows, w), dt)
    return pl.pallas_call(
        functools.partial(_ml_proj_kernel, heads=heads, first_last=first_last,
                          k_scale=(inner // heads) ** -0.5),
        out_shape=(tok(inner, BF16), jax.ShapeDtypeStruct((inner, rows), BF16), tok(inner, BF16),
                   tok(inner, BF16), tok(inner, BF16), tok(128, F32),
                   jax.ShapeDtypeStruct((ng, rows), F32)),
        grid=(rows // tm,),
        in_specs=[pl.BlockSpec((tm, d), lambda i: (i, 0)),
                  pl.BlockSpec((halo, d), lambda i: (jnp.maximum(i * r8 - 1, 0), 0)),
                  pl.BlockSpec((halo, d), lambda i: (jnp.minimum((i + 1) * r8, last8), 0)),
                  pl.BlockSpec((1,) + mods.shape[1:], lambda i: (lay.mod_row(i, tm), 0, 0)),
                  _const_spec((1, d)),
                  pl.BlockSpec((d, inner), lambda i: (0, 0), pipeline_mode=pl.Buffered(1)),
                  pl.BlockSpec((d, inner), lambda i: (0, 1), pipeline_mode=pl.Buffered(1)),
                  _const_spec(w_gt.shape),
                  _const_spec(gb_r.shape), _const_spec(conv_w.shape), _const_spec((1, inner)),
                  _const_spec(wq_t.shape), _const_spec(wk_t.shape), _const_spec(wv_t.shape)],
        out_specs=tuple([pl.BlockSpec((tm, inner), lambda i: (i, 0)),
                         pl.BlockSpec((inner, tm), lambda i: (0, i))]
                        + [pl.BlockSpec((tm, inner), lambda i: (i, 0))] * 3
                        + [pl.BlockSpec((tm, 128), lambda i: (i, 0)),
                           pl.BlockSpec((ng, tm), lambda i: (0, i))]),
        scratch_shapes=[pltpu.VMEM((inner // ML_COL_CHUNK, tm + 2 * halo, ML_COL_CHUNK), F32)],
        compiler_params=_cparams(("arbitrary",)),
        name="mlstm_in_proj",
    )(stream, stream, stream, mods, g.reshape(1, d), w_all, w_all, w_gt, gb_r,
      conv_w, conv_b.reshape(1, inner), wq_t, wk_t, wv_t)


def _ml_scan_kernel(qf_ref, kf_ref, vf_ref, gcf_ref, grf_ref, qb_ref, kb_ref, vb_ref, gcb_ref,
                    grb_ref, hf_ref, hb_ref, c_ref, cb_ref, m_ref, *, heads):
    i = pl.program_id(1)
    dh = qf_ref.shape[1] // heads
    cols = c_ref.shape[3]
    cw = 256

    @pl.when(i == 0)
    def _():
        c_ref[...] = jnp.zeros(c_ref.shape, F32)
        cb_ref[...] = jnp.zeros(cb_ref.shape, BF16)
        m_ref[...] = jnp.zeros(m_ref.shape, F32)

    ii = lax.broadcasted_iota(jnp.int32, (CHUNK, CHUNK), 0)
    jj = lax.broadcasted_iota(jnp.int32, (CHUNK, CHUNK), 1)
    lane0 = lax.broadcasted_iota(jnp.int32, (CHUNK, cols - dh), 1) == 0
    one_col = jnp.where(lane0, 1.0, 0.0).astype(BF16)
    tok_r = lax.broadcasted_iota(jnp.int32, (1, CHUNK), 1)
    dir_refs = ((qf_ref, kf_ref, vf_ref, gcf_ref, grf_ref, hf_ref),
                (qb_ref, kb_ref, vb_ref, gcb_ref, grb_ref, hb_ref))
    for h in range(heads):
        for d, (q_ref, k_ref, v_ref, gc_ref, gr_ref, h_ref) in enumerate(dir_refs):
            mask = (jj <= ii) if d == 0 else (jj >= ii)
            end = CHUNK - 1 if d == 0 else 0
            hs = slice(h * dh, (h + 1) * dh)
            ki, kb = (2 * d) * heads + h, (2 * d + 1) * heads + h
            q, kt, v = q_ref[:, hs], k_ref[hs, :], v_ref[:, hs]
            b_c = gc_ref[:, kb:kb + 1]
            ig_r = gr_ref[ki:ki + 1, :]
            b_r = gr_ref[kb:kb + 1, :]
            m_prev = m_ref[d, h, 0:1, 0:1]
            log_w = jnp.where(mask, b_c - b_r + ig_r, -jnp.inf)
            log_prev = b_c + m_prev
            m_row = jnp.maximum(log_prev, jnp.max(log_w, axis=-1, keepdims=True))
            w = jnp.exp(log_w - m_row)
            w_prev = jnp.exp(log_prev - m_row)
            s = _dot(q, kt) * w
            qn = _dot(q, cb_ref[d, h, :, dh:cols])
            qn = jnp.sum(jnp.where(lane0, qn, 0.0), axis=-1, keepdims=True)
            den = jnp.sum(s, axis=-1, keepdims=True) + w_prev * qn
            floor = jnp.maximum(jnp.abs(den), jnp.exp(-m_row))
            s_b = s.astype(BF16)
            for c0 in range(0, dh, cw):
                vs = slice(h * dh + c0, h * dh + c0 + cw)
                num = _dot(s_b, v_ref[:, vs]) + w_prev * _dot(q, cb_ref[d, h, :, c0:c0 + cw])
                h_ref[:, vs] = (num / floor).astype(h_ref.dtype)
            b_last = jnp.sum(jnp.where(tok_r == end, b_r, 0.0), axis=-1, keepdims=True)
            log_k = b_last - b_r + ig_r
            m_new = jnp.maximum(b_last + m_prev, jnp.max(log_k, axis=-1, keepdims=True))
            w_k = jnp.exp(log_k - m_new)
            w_c = jnp.exp(b_last + m_prev - m_new)
            kwt = (kt.astype(F32) * w_k).astype(BF16)
            for c0 in range(0, cols, cw):
                c1 = min(c0 + cw, cols)
                rhs = v_ref[:, h * dh + c0:h * dh + c1] if c1 <= dh else one_col
                cnew = w_c * c_ref[d, h, :, c0:c1] + _dot(kwt, rhs)
                c_ref[d, h, :, c0:c1] = cnew
                cb_ref[d, h, :, c0:c1] = cnew.astype(BF16)
            m_ref[d, h] = jnp.broadcast_to(m_new, m_ref.shape[2:])


def _ml_scan(lay, q, k, v, gc, gr, heads):
    inner = q.shape[1]
    dh = inner // heads
    ncc, nlc = lay.ctx_len // CHUNK, lay.seq // CHUNK
    ng = gr.shape[0]
    ins = []
    for f in (lay.chunk_fwd, lay.chunk_bwd):
        tok = pl.BlockSpec((CHUNK, inner), lambda b, i, f=f: (f(b, i), 0))
        ins += [tok, pl.BlockSpec((inner, CHUNK), lambda b, i, f=f: (0, f(b, i))), tok]
        ins += [pl.BlockSpec((CHUNK, 128), lambda b, i, f=f: (f(b, i), 0)),
                pl.BlockSpec((ng, CHUNK), lambda b, i, f=f: (0, f(b, i)))]
    outs = tuple(pl.BlockSpec((CHUNK, inner), lambda b, i, f=f: (f(b, i), 0))
                 for f in (lay.chunk_fwd, lay.chunk_bwd))
    return pl.pallas_call(
        functools.partial(_ml_scan_kernel, heads=heads),
        out_shape=(jax.ShapeDtypeStruct((lay.rows, inner), BF16),) * 2,
        grid=(lay.batch, ncc + nlc),
        in_specs=ins,
        out_specs=outs,
        scratch_shapes=[pltpu.VMEM((2, heads, dh, dh + 128), F32),
                        pltpu.VMEM((2, heads, dh, dh + 128), BF16),
                        pltpu.VMEM((2, heads, 8, 128), F32)],
        compiler_params=_cparams(("arbitrary", "arbitrary")),
        name="mlstm_scan",
    )(q, k, v, gc, gr, q, k, v, gc, gr)


def _ml_tail_kernel(x_ref, m_ref, hf_ref, hb_ref, xc_ref, op_ref, gn_ref, sk_ref, wout_ref,
                    g_ref, w1_ref, w2_ref, fg_ref, o_ref, act_ref, *, head_dim, hc):
    i = pl.program_id(0)
    fill, drain = i % 2, 1 - i % 2
    tm = x_ref.shape[0]
    n_chunks = w1_ref.shape[1] // hc
    rc = tm // n_chunks

    @pl.when(i == 0)
    def _():
        act_ref[...] = jnp.zeros(act_ref.shape, act_ref.dtype)

    def prepare(c):
        rs = slice(c * rc, (c + 1) * rc)
        hsum = hf_ref[rs, :].astype(F32) + hb_ref[rs, :].astype(F32)
        hn = _group_norm(hsum, head_dim) * gn_ref[...] + sk_ref[...] * xc_ref[rs, :].astype(F32)
        act_ref[fill, rs, :] = (_sigmoid(op_ref[rs, :].astype(F32)) * hn).astype(act_ref.dtype)

    m = m_ref[0]
    lat = x_ref[...] + m[2:3] * _dot(act_ref[drain], wout_ref[...])
    h = _rms_mod(lat, g_ref[...], m[3:4], m[4:5]).astype(BF16)
    acc = jnp.zeros(lat.shape, F32)
    for c in range(n_chunks):
        a = _dot(h, w1_ref[:, c * hc:(c + 1) * hc])
        a = jnp.square(jnp.maximum(a, 0.0)).astype(BF16)
        acc = acc + _dot(a, w2_ref[c * hc:(c + 1) * hc, :])
        prepare(c)
    y = lat + m[5:6] * acc
    o_ref[...] = y * lax.rsqrt(jnp.mean(y * y, axis=-1, keepdims=True) + EPS) * fg_ref[...]


def _ml_tail(lay, stream, mods, hf, hb, xc, o_pre, gn_g, skip, w_out, g, w1, w2, layer, final_g,
             heads):
    d = stream.shape[1]
    inner = hf.shape[1]
    tm = ROW_TILE
    sk = lay.ctx_rows // tm
    n_tiles = lay.rows // tm - sk
    per_b = lay.seq // tm

    def cur(w):
        return pl.BlockSpec((tm, w), lambda i: (jnp.minimum(i, n_tiles - 1) + sk, 0))

    def prev(i):
        return jnp.maximum(i - 1, 0)

    return pl.pallas_call(
        functools.partial(_ml_tail_kernel, head_dim=inner // heads, hc=512),
        out_shape=jax.ShapeDtypeStruct((n_tiles * tm, d), F32),
        grid=(n_tiles + 1,),
        in_specs=[pl.BlockSpec((tm, d), lambda i: (prev(i) + sk, 0)),
                  pl.BlockSpec((1,) + mods.shape[1:], lambda i: (prev(i) // per_b, 0, 0)),
                  cur(inner), cur(inner), cur(inner), cur(inner),
                  _const_spec((1, inner)), _const_spec((1, inner)), _const_spec(w_out.shape),
                  _const_spec((1, d)), _layer_spec(w1, layer), _layer_spec(w2, layer),
                  _const_spec((1, d))],
        out_specs=pl.BlockSpec((tm, d), lambda i: (prev(i), 0)),
        scratch_shapes=[pltpu.VMEM((2, tm, inner), BF16)],
        compiler_params=_cparams(("arbitrary",)),
        name="mlstm_out_mlp_final",
    )(stream, mods, hf, hb, xc, o_pre, gn_g.reshape(1, inner), skip.reshape(1, inner), w_out,
      g.reshape(1, d), w1, w2, final_g.reshape(1, d))


def kernel(x, c, ctx, c_ctx, mod_w, mod_b, norm_mix_g, norm_mlp_g, mlp_w1, mlp_w2, final_norm_g,
           ev_w_in, ev_w_out, s5_lambda_re, s5_lambda_im, s5_log_step, s5_b_re, s5_b_im, s5_c_re,
           s5_c_im, s5_d, s5_w_glu, s5_b_glu, ret_decay_logit, ret_gn_g,
           ml_w_in, ml_gate_b, ml_conv_w, ml_conv_b, ml_wq, ml_wk, ml_wv, ml_gn_g, ml_skip,
           ml_w_out):
    bsz, seq, d = x.shape
    ctx_len = ctx.shape[1]
    depth = mod_w.shape[0]
    assert depth == 2, "one even (S5 + retention) layer followed by one mLSTM layer"
    lay = _Layout(bsz, ctx_len, seq)
    assert lay.ctx_rows % ROW_TILE == 0 and seq % ROW_TILE == 0
    assert ctx_len % (S5_SEGS * S5_TB) == 0 and seq % (S5_SEGS * S5_TB) == 0
    assert ctx_len % CHUNK == 0 and seq % CHUNK == 0

    c_all = jnp.concatenate([c, c_ctx[None, :], jnp.zeros((8 - bsz - 1, d), F32)], axis=0)
    mods = _mod_table(c_all, mod_w, mod_b)
    ctx2d, x2d = ctx.reshape(bsz * ctx_len, d), x.reshape(bsz * seq, d)

    ret_w = ret_gn_g.shape[1]
    ret_heads = ret_decay_logit.shape[2]
    u_ctx, u_lat, qkv, gate = _proj(lay, ctx2d, x2d, mods[0], norm_mix_g[0],
                                    ev_w_in[0].astype(BF16), s5_d.shape[1], ret_heads)
    y_parts = _s5_mixer(lay, u_ctx, u_lat, s5_lambda_re[0], s5_lambda_im[0], s5_log_step[0],
                        s5_b_re[0], s5_b_im[0], s5_c_re[0], s5_c_im[0])
    rf, rb = _retention(lay, qkv, ret_decay_logit[0])
    stream = _even_out(lay, ctx2d, x2d, mods[0], (u_ctx, u_lat), y_parts, gate, rf, rb, s5_d[0],
                       s5_w_glu[0].astype(BF16), s5_b_glu[0], ret_gn_g[0],
                       ev_w_out[0].astype(BF16), ret_heads)
    w1_all, w2_all = mlp_w1.astype(BF16), mlp_w2.astype(BF16)
    stream = _mlp(lay, stream, mods[0], norm_mlp_g[0], w1_all, w2_all, 0)

    ml_heads = ml_gate_b.shape[1] // 4
    q, k, v, o_pre, xc, gc, gr = _ml_proj(lay, stream, mods[1], norm_mix_g[1], ml_w_in[0],
                                          ml_gate_b[0], ml_conv_w[0], ml_conv_b[0],
                                          ml_wq[0], ml_wk[0], ml_wv[0], ml_heads)
    hf, hb = _ml_scan(lay, q, k, v, gc, gr, ml_heads)
    out = _ml_tail(lay, stream, mods[1], hf, hb, xc, o_pre, ml_gn_g[0], ml_skip[0],
                   ml_w_out[0].astype(BF16), norm_mlp_g[1], w1_all, w2_all, 1, final_norm_g,
                   ml_heads)
    return out.reshape(bsz, seq, d)
```
